```python
import math
import jax, jax.numpy as jnp
from jax import lax
import numpy as np

D_MODEL = 1024
BATCH = 16
SEQ = 2048
DEPTH = 2
DEC_BATCH = 8
DEC_SEQ = 64
PAST_LEN = 1024

CHUNK = 64
GLA_HEADS = 4
GLA_DK = D_MODEL // 2 // GLA_HEADS
GLA_DV = D_MODEL // GLA_HEADS
GLA_RANK = 16
GLA_GATE_NORM = 16.0
GLA_NORM_EPS = 1e-5
RWKV_HEAD = 64
RWKV_DIM = D_MODEL
RWKV_HEADS = RWKV_DIM // RWKV_HEAD
RWKV_W_RANK = 64
RWKV_A_RANK = 64
RWKV_G_RANK = 128
RWKV_GN_EPS = 64e-5
SSM_INNER = 2 * D_MODEL
SSM_HEADDIM = 64
SSM_HEADS = SSM_INNER // SSM_HEADDIM
SSM_GROUPS = 4
SSM_STATE = 128
SSM_CONV = 4
SSM_CONV_DIM = SSM_INNER + 2 * SSM_GROUPS * SSM_STATE
SSM_NORM_EPS = 1e-5
N_EXPERTS = 64
TOP_K = 8
N_EXPERT_GROUPS = 8
TOPK_GROUPS = 4
D_EXPERT = D_MODEL // 4
D_SHARED = D_EXPERT
ROUTED_SCALE = 2.5
MOE_BLOCK = 256
DN_ALPHA = (2.0 * DEPTH) ** 0.25
DN_BETA = (8.0 * DEPTH) ** -0.25
LN_EPS = 1e-5
N_GATE = 3 * D_MODEL
GLA_PROJ = 2 * GLA_HEADS * GLA_DK + 2 * GLA_HEADS * GLA_DV + GLA_RANK
RWKV_PROJ = 3 * RWKV_DIM + RWKV_W_RANK + RWKV_A_RANK + RWKV_G_RANK
SSM_PROJ = SSM_INNER + SSM_CONV_DIM + SSM_HEADS
N_IN = N_GATE + GLA_PROJ + RWKV_PROJ + SSM_PROJ

kernel_name = 'hybrid_gla_rwkv7_ssd_moe_stream'


def _split(u, sizes):
    offs = np.cumsum(sizes)[:-1].tolist()
    return jnp.split(u, offs, axis=-1)


def _layer_norm(x, g, b):
    xf = x.astype(jnp.float32)
    mu = jnp.mean(xf, -1, keepdims=True)
    var = jnp.mean(jnp.square(xf - mu), -1, keepdims=True)
    return (xf - mu) * lax.rsqrt(var + LN_EPS) * g + b


def _rms_norm(x, g, eps):
    xf = x.astype(jnp.float32)
    return xf * lax.rsqrt(jnp.mean(xf * xf, -1, keepdims=True) + eps) * g


def _causal_conv(u, buf, w, bias):
    l = u.shape[1]
    full = jnp.concatenate([buf.astype(u.dtype), u], axis=1)
    out = lax.conv_general_dilated(full, w[:, None, :].astype(u.dtype), window_strides=(1,), padding='VALID',
                                   dimension_numbers=('NWC', 'WIO', 'NWC'),
                                   feature_group_count=w.shape[1])
    return out + bias, full[:, l:]


def _gla_chunked(q, k, v, log_a, s0):
    b, l, h, dk = q.shape
    dv = v.shape[-1]
    c = min(CHUNK, l)
    n = l // c

    def blocks(t):
        return t.astype(jnp.float32).reshape(b, n, c, h, t.shape[-1]).transpose(0, 3, 1, 2, 4)

    q, k, v, g = blocks(q), blocks(k), blocks(v), blocks(log_a)
    g = jnp.cumsum(g, axis=3)
    g_last = g[:, :, :, -1:]
    q_dec = q * jnp.exp(g) * dk ** -0.5
    k_inv = k * jnp.exp(-g)
    k_end = k * jnp.exp(g_last - g)
    causal = jnp.tril(jnp.ones((c, c), dtype=bool))
    scores = jnp.where(causal, jnp.einsum('bhnqd,bhnsd->bhnqs', q_dec, k_inv), 0.0)
    o_intra = jnp.einsum('bhnqs,bhnsv->bhnqv', scores, v)
    chunk_kv = jnp.einsum('bhnsd,bhnsv->nbhdv', k_end, v)
    chunk_decay = jnp.exp(g_last[:, :, :, 0]).transpose(2, 0, 1, 3)

    def step(s, inp):
        kv, dec = inp
        return s * dec[..., None] + kv, s

    s_final, s_prev = lax.scan(step, s0.astype(jnp.float32), (chunk_kv, chunk_decay))
    o_inter = jnp.einsum('bhnqd,nbhdv->bhnqv', q_dec, s_prev)
    o = (o_intra + o_inter).transpose(0, 2, 3, 1, 4).reshape(b, l, h, dv)
    return o, s_final


def _gla_mixer(u, s0, wa2, ba, norm_g):
    b, l, _ = u.shape
    q, k, v, r, a_low = _split(u, (GLA_HEADS * GLA_DK, GLA_HEADS * GLA_DK, GLA_HEADS * GLA_DV,
                                   GLA_HEADS * GLA_DV, GLA_RANK))
    log_a = jax.nn.log_sigmoid((a_low @ wa2 + ba).astype(jnp.float32)) / GLA_GATE_NORM
    hk = lambda t: t.reshape(b, l, GLA_HEADS, GLA_DK)
    o, s_new = _gla_chunked(hk(q), hk(k), v.reshape(b, l, GLA_HEADS, GLA_DV), hk(log_a), s0)
    o = _rms_norm(o, norm_g, GLA_NORM_EPS).reshape(b, l, GLA_HEADS * GLA_DV)
    return o * jax.nn.silu(r.astype(jnp.float32)), s_new


def _rwkv7_scan(r, w, k, v, a, bb, s0):
    def step(s, inp):
        rt, wt, kt, vt, at, bt = inp
        sa = jnp.einsum('bhvk,bhk->bhv', s, at)
        s = s * wt[:, :, None, :] + sa[..., None] * bt[:, :, None, :] + vt[..., None] * kt[:, :, None, :]
        return s, jnp.einsum('bhvk,bhk->bhv', s, rt)

    xs = tuple(t.astype(jnp.float32).transpose(1, 0, 2, 3) for t in (r, w, k, v, a, bb))
    s_final, ys = lax.scan(step, s0.astype(jnp.float32), xs)
    return ys.transpose(1, 0, 2, 3), s_final


def _rwkv7_mixer(u, shift_buf, s0, mu, w0, w2, a0, a2, g2, k_k, k_a, r_k, ln_g, ln_b):
    b, l, _ = u.shape
    full = jnp.concatenate([shift_buf.astype(u.dtype), u], axis=1)
    us = u + (full[:, :-1] - u) * mu
    r, k, v, w_low, a_low, g_low = _split(us, (RWKV_DIM, RWKV_DIM, RWKV_DIM,
                                               RWKV_W_RANK, RWKV_A_RANK, RWKV_G_RANK))
    log_w = -jax.nn.softplus(-(w0 + jnp.tanh(w_low) @ w2).astype(jnp.float32)) - 0.5
    decay = jnp.exp(-jnp.exp(log_w))
    a = jax.nn.sigmoid((a0 + a_low @ a2).astype(jnp.float32))
    gate = jax.nn.sigmoid(g_low) @ g2
    heads = lambda t: t.reshape(b, l, RWKV_HEADS, RWKV_HEAD)
    kk = heads(k * k_k).astype(jnp.float32)
    kk = kk * lax.rsqrt(jnp.maximum(jnp.sum(kk * kk, -1, keepdims=True), 1e-24))
    k = k * (1.0 + (a - 1.0) * k_a)
    y, s_new = _rwkv7_scan(heads(r), heads(decay), heads(k), heads(v), -kk, kk * heads(a), s0)
    mean = jnp.mean(y, -1, keepdims=True)
    var = jnp.mean(jnp.square(y - mean), -1, keepdims=True)
    y = ((y - mean) * lax.rsqrt(var + RWKV_GN_EPS)).reshape(b, l, RWKV_DIM) * ln_g + ln_b
    bonus = jnp.sum(heads(r) * heads(k) * r_k, -1, keepdims=True) * heads(v)
    y = (y + bonus.reshape(b, l, RWKV_DIM)) * gate
    return y, full[:, -1:], s_new


def _ssd_chunked(x, dt, a, bm, cm, s0):
    b, l, h, p = x.shape
    g, nst = bm.shape[2], bm.shape[3]
    j = h // g
    c = min(CHUNK, l)
    n = l // c
    f32 = jnp.float32
    dt = dt.astype(f32)
    acum = jnp.cumsum((dt * a).reshape(b, n, c, h).transpose(0, 3, 1, 2), axis=-1)
    xdt = (x.astype(f32) * dt[..., None]).reshape(b, n, c, g, j, p)
    bm = bm.astype(f32).reshape(b, n, c, g, nst)
    cm = cm.astype(f32).reshape(b, n, c, g, nst)
    causal = jnp.tril(jnp.ones((c, c), dtype=bool))
    seg = acum[..., :, None] - acum[..., None, :]
    decay = jnp.where(causal, jnp.exp(jnp.where(causal, seg, 0.0)), 0.0).reshape(b, g, j, n, c, c)
    cb = jnp.einsum('bnqgd,bnsgd->bgnqs', cm, bm)
    y_diag = jnp.einsum('bgjnqs,bnsgjp->bnqgjp', cb[:, :, None] * decay, xdt)
    a_last = acum[..., -1]
    dec_end = jnp.exp(a_last[..., None] - acum).reshape(b, g, j, n, c).transpose(0, 3, 4, 1, 2)
    chunk_states = jnp.einsum('bnsgd,bnsgjp->nbgjpd', bm, xdt * dec_end[..., None]).reshape(n, b, h, p, nst)
    chunk_decay = jnp.exp(a_last).transpose(2, 0, 1)

    def step(s, inp):
        st, dec = inp
        return s * dec[..., None, None] + st, s

    s_final, s_prev = lax.scan(step, s0.astype(f32), (chunk_states, chunk_decay))
    dec_in = jnp.exp(acum).reshape(b, g, j, n, c).transpose(0, 3, 4, 1, 2)
    y_off = jnp.einsum('bnqgd,nbgjpd->bnqgjp', cm, s_prev.reshape(n, b, g, j, p, nst)) * dec_in[..., None]
    return (y_diag + y_off).reshape(b, l, h, p), s_final


def _mamba2_mixer(u, conv_buf, s0, conv_w, conv_b, dt_bias, a_log, d_skip, norm_g):
    b, l, _ = u.shape
    z, xbc, dt = _split(u, (SSM_INNER, SSM_CONV_DIM, SSM_HEADS))
    xbc, conv_new = _causal_conv(xbc, conv_buf, conv_w, conv_b)
    xbc = jax.nn.silu(xbc)
    xs, bm, cm = _split(xbc, (SSM_INNER, SSM_GROUPS * SSM_STATE, SSM_GROUPS * SSM_STATE))
    xs = xs.reshape(b, l, SSM_HEADS, SSM_HEADDIM)
    dt = jax.nn.softplus((dt + dt_bias).astype(jnp.float32))
    a = -jnp.exp(a_log.astype(jnp.float32))
    y, s_new = _ssd_chunked(xs, dt, a, bm.reshape(b, l, SSM_GROUPS, SSM_STATE),
                            cm.reshape(b, l, SSM_GROUPS, SSM_STATE), s0)
    y = (y + xs * d_skip[:, None]).reshape(b, l, SSM_INNER) * jax.nn.silu(z.astype(jnp.float32))
    y = _rms_norm(y.reshape(b, l, SSM_GROUPS, SSM_INNER // SSM_GROUPS),
                  norm_g.reshape(SSM_GROUPS, SSM_INNER // SSM_GROUPS), SSM_NORM_EPS)
    return y.reshape(b, l, SSM_INNER), conv_new, s_new


def _moe(x, w_router, router_bias, exp_w1, exp_w3, exp_w2, sh_w1, sh_w3, sh_w2):
    b, l, d = x.shape
    t = x.reshape(b * l, d)
    s = jax.nn.sigmoid((t @ w_router).astype(jnp.float32))
    sb = s + router_bias
    per_group = N_EXPERTS // N_EXPERT_GROUPS
    gscore = jnp.sum(lax.top_k(sb.reshape(-1, N_EXPERT_GROUPS, per_group), 2)[0], -1)
    _, gidx = lax.top_k(gscore, TOPK_GROUPS)
    gmask = jnp.sum(jax.nn.one_hot(gidx, N_EXPERT_GROUPS, dtype=jnp.float32), 1)
    emask = jnp.repeat(gmask, per_group, axis=1) > 0
    _, eidx = lax.top_k(jnp.where(emask, sb, -jnp.inf), TOP_K)
    wsel = jnp.take_along_axis(s, eidx, axis=1)
    wsel = wsel / jnp.sum(wsel, -1, keepdims=True) * ROUTED_SCALE
    gates = jnp.einsum('tk,tke->te', wsel, jax.nn.one_hot(eidx, N_EXPERTS, dtype=jnp.float32)).astype(t.dtype)
    n_tok = t.shape[0]
    n_blk = -(-n_tok // MOE_BLOCK)
    pad = n_blk * MOE_BLOCK - n_tok
    tp = jnp.pad(t, ((0, pad), (0, 0))).reshape(n_blk, MOE_BLOCK, d)
    gp = jnp.pad(gates, ((0, pad), (0, 0))).reshape(n_blk, MOE_BLOCK, N_EXPERTS)

    def block(args):
        tb, gb = args
        hid = jax.nn.silu(jnp.einsum('td,edf->tef', tb, exp_w1)) * jnp.einsum('td,edf->tef', tb, exp_w3)
        return jnp.einsum('tef,efd->td', hid * gb[..., None], exp_w2)

    routed = lax.map(block, (tp, gp)).reshape(n_blk * MOE_BLOCK, d)[:n_tok]
    shared = (jax.nn.silu(t @ sh_w1) * (t @ sh_w3)) @ sh_w2
    return (routed + shared).reshape(b, l, d)


def _layer(x, st_gla, st_rwkv, st_shift, st_ssm, st_conv,
           w_in, gla_wa2, gla_ba, gla_norm_g, gla_wo,
           rwkv_mu, rwkv_w0, rwkv_w2, rwkv_a0, rwkv_a2, rwkv_g2, rwkv_k_k, rwkv_k_a, rwkv_r_k,
           rwkv_ln_g, rwkv_ln_b, rwkv_wo,
           ssm_conv_w, ssm_conv_b, ssm_dt_bias, ssm_a_log, ssm_d, ssm_norm_g, ssm_wo,
           w_out, ln1_g, ln1_b,
           w_router, router_bias, exp_w1, exp_w3, exp_w2, sh_w1, sh_w3, sh_w2, ln2_g, ln2_b):
    b, l, d = x.shape
    u = x @ w_in
    u_gate, u_gla, u_rwkv, u_ssm = _split(u, (N_GATE, GLA_PROJ, RWKV_PROJ, SSM_PROJ))
    o_gla, s_gla = _gla_mixer(u_gla, st_gla, gla_wa2, gla_ba, gla_norm_g)
    o_rwkv, shift_new, s_rwkv = _rwkv7_mixer(u_rwkv, st_shift, st_rwkv, rwkv_mu, rwkv_w0, rwkv_w2, rwkv_a0,
                                             rwkv_a2, rwkv_g2, rwkv_k_k, rwkv_k_a, rwkv_r_k, rwkv_ln_g, rwkv_ln_b)
    o_ssm, conv_new, s_ssm = _mamba2_mixer(u_ssm, st_conv, st_ssm, ssm_conv_w, ssm_conv_b, ssm_dt_bias,
                                           ssm_a_log, ssm_d, ssm_norm_g)
    g = jax.nn.sigmoid(u_gate.astype(jnp.float32)).reshape(b, l, 3, d)
    merged = g[:, :, 0] * (o_gla @ gla_wo) + g[:, :, 1] * (o_rwkv @ rwkv_wo) + g[:, :, 2] * (o_ssm @ ssm_wo)
    h = _layer_norm(DN_ALPHA * x + merged @ w_out, ln1_g, ln1_b).astype(x.dtype)
    h = _layer_norm(DN_ALPHA * h + _moe(h, w_router, router_bias, exp_w1, exp_w3, exp_w2, sh_w1, sh_w3, sh_w2),
                    ln2_g, ln2_b).astype(x.dtype)
    return h, s_gla, s_rwkv, shift_new, s_ssm, conv_new


def _zero_states(batch, dtype):
    return (jnp.zeros((batch, GLA_HEADS, GLA_DK, GLA_DV), jnp.float32),
            jnp.zeros((batch, RWKV_HEADS, RWKV_HEAD, RWKV_HEAD), jnp.float32),
            jnp.zeros((batch, 1, RWKV_PROJ), dtype),
            jnp.zeros((batch, SSM_HEADS, SSM_HEADDIM, SSM_STATE), jnp.float32),
            jnp.zeros((batch, SSM_CONV - 1, SSM_CONV_DIM), dtype))


def setup_inputs(seed: int = 0) -> dict:
    key = jax.random.key(seed)
    ks = iter(jax.random.split(key, 64))
    f32 = jnp.float32
    L = DEPTH

    def nrm(shape, scale):
        return jax.random.normal(next(ks), shape, f32) * scale

    def near_one(shape):
        return 1.0 + nrm(shape, 0.1)

    dt0 = jnp.exp(jax.random.uniform(next(ks), (L, SSM_HEADS), f32, math.log(1e-3), math.log(1e-1)))
    a_init = jax.random.uniform(next(ks), (L, SSM_HEADS), f32, 1.0, 16.0)
    mu_init = jax.random.uniform(next(ks), (L, RWKV_PROJ), f32)
    return {
        'x_prompt': nrm((BATCH, SEQ, D_MODEL), 1.0),
        'x_sample': nrm((DEC_BATCH, DEC_SEQ, D_MODEL), 1.0),
        'state_gla': nrm((L, DEC_BATCH, GLA_HEADS, GLA_DK, GLA_DV), 0.5),
        'state_rwkv': nrm((L, DEC_BATCH, RWKV_HEADS, RWKV_HEAD, RWKV_HEAD), 0.3),
        'state_rwkv_shift': nrm((L, DEC_BATCH, 1, RWKV_PROJ), 1.0),
        'state_ssm': nrm((L, DEC_BATCH, SSM_HEADS, SSM_HEADDIM, SSM_STATE), 0.3),
        'state_ssm_conv': nrm((L, DEC_BATCH, SSM_CONV - 1, SSM_CONV_DIM), 1.0),
        'ln_in_g': near_one((D_MODEL,)),
        'ln_in_b': nrm((D_MODEL,), 0.02),
        'w_in': nrm((L, D_MODEL, N_IN), D_MODEL ** -0.5),
        'gla_wa2': nrm((L, GLA_RANK, GLA_HEADS * GLA_DK), GLA_RANK ** -0.5),
        'gla_ba': nrm((L, GLA_HEADS * GLA_DK), 0.1),
        'gla_norm_g': near_one((L, GLA_DV)),
        'gla_wo': nrm((L, GLA_HEADS * GLA_DV, D_MODEL), (GLA_HEADS * GLA_DV) ** -0.5),
        'rwkv_mu': mu_init,
        'rwkv_w0': -1.0 + nrm((L, RWKV_DIM), 0.5),
        'rwkv_w2': nrm((L, RWKV_W_RANK, RWKV_DIM), 0.5 * RWKV_W_RANK ** -0.5),
        'rwkv_a0': nrm((L, RWKV_DIM), 0.1),
        'rwkv_a2': nrm((L, RWKV_A_RANK, RWKV_DIM), 0.5 * RWKV_A_RANK ** -0.5),
        'rwkv_g2': nrm((L, RWKV_G_RANK, RWKV_DIM), RWKV_G_RANK ** -0.5),
        'rwkv_k_k': 0.85 + nrm((L, RWKV_DIM), 0.05),
        'rwkv_k_a': near_one((L, RWKV_DIM)),
        'rwkv_r_k': nrm((L, RWKV_HEADS, RWKV_HEAD), 0.1),
        'rwkv_ln_g': near_one((L, RWKV_DIM)),
        'rwkv_ln_b': nrm((L, RWKV_DIM), 0.02),
        'rwkv_wo': nrm((L, RWKV_DIM, D_MODEL), RWKV_DIM ** -0.5),
        'ssm_conv_w': nrm((L, SSM_CONV, SSM_CONV_DIM), SSM_CONV ** -0.5),
        'ssm_conv_b': nrm((L, SSM_CONV_DIM), 0.02),
        'ssm_dt_bias': dt0 + jnp.log(-jnp.expm1(-dt0)),
        'ssm_a_log': jnp.log(a_init),
        'ssm_d': near_one((L, SSM_HEADS)),
        'ssm_norm_g': near_one((L, SSM_INNER)),
        'ssm_wo': nrm((L, SSM_INNER, D_MODEL), SSM_INNER ** -0.5),
        'w_out': nrm((L, D_MODEL, D_MODEL), DN_BETA * D_MODEL ** -0.5),
        'ln1_g': near_one((L, D_MODEL)),
        'ln1_b': nrm((L, D_MODEL), 0.02),
        'w_router': nrm((L, D_MODEL, N_EXPERTS), D_MODEL ** -0.5),
        'router_bias': nrm((L, N_EXPERTS), 0.01),
        'exp_w1': nrm((L, N_EXPERTS, D_MODEL, D_EXPERT), D_MODEL ** -0.5),
        'exp_w3': nrm((L, N_EXPERTS, D_MODEL, D_EXPERT), D_MODEL ** -0.5),
        'exp_w2': nrm((L, N_EXPERTS, D_EXPERT, D_MODEL), DN_BETA * D_EXPERT ** -0.5),
        'sh_w1': nrm((L, D_MODEL, D_SHARED), D_MODEL ** -0.5),
        'sh_w3': nrm((L, D_MODEL, D_SHARED), D_MODEL ** -0.5),
        'sh_w2': nrm((L, D_SHARED, D_MODEL), DN_BETA * D_SHARED ** -0.5),
        'ln2_g': near_one((L, D_MODEL)),
        'ln2_b': nrm((L, D_MODEL), 0.02),
    }


def reference(x_prompt, x_sample, state_gla, state_rwkv, state_rwkv_shift, state_ssm, state_ssm_conv,
              ln_in_g, ln_in_b, w_in, gla_wa2, gla_ba, gla_norm_g, gla_wo,
              rwkv_mu, rwkv_w0, rwkv_w2, rwkv_a0, rwkv_a2, rwkv_g2, rwkv_k_k, rwkv_k_a, rwkv_r_k,
              rwkv_ln_g, rwkv_ln_b, rwkv_wo,
              ssm_conv_w, ssm_conv_b, ssm_dt_bias, ssm_a_log, ssm_d, ssm_norm_g, ssm_wo,
              w_out, ln1_g, ln1_b,
              w_router, router_bias, exp_w1, exp_w3, exp_w2, sh_w1, sh_w3, sh_w2, ln2_g, ln2_b):
    layer_params = (w_in, gla_wa2, gla_ba, gla_norm_g, gla_wo,
                    rwkv_mu, rwkv_w0, rwkv_w2, rwkv_a0, rwkv_a2, rwkv_g2, rwkv_k_k, rwkv_k_a, rwkv_r_k,
                    rwkv_ln_g, rwkv_ln_b, rwkv_wo,
                    ssm_conv_w, ssm_conv_b, ssm_dt_bias, ssm_a_log, ssm_d, ssm_norm_g, ssm_wo,
                    w_out, ln1_g, ln1_b,
                    w_router, router_bias, exp_w1, exp_w3, exp_w2, sh_w1, sh_w3, sh_w2, ln2_g, ln2_b)
    h_p = _layer_norm(x_prompt, ln_in_g, ln_in_b).astype(x_prompt.dtype)
    h_s = _layer_norm(x_sample, ln_in_g, ln_in_b).astype(x_sample.dtype)
    new_p, new_s = [], []
    for i in range(DEPTH):
        lp = [t[i] for t in layer_params]
        h_p, *st_p = _layer(h_p, *_zero_states(h_p.shape[0], h_p.dtype), *lp)
        new_p.append(st_p)
        h_s, *st_s = _layer(h_s, state_gla[i], state_rwkv[i], state_rwkv_shift[i], state_ssm[i],
                            state_ssm_conv[i], *lp)
        new_s.append(st_s)
    gla_p, rwkv_p, shift_p, ssm_p, conv_p = (jnp.stack(z) for z in zip(*new_p))
    gla_s, rwkv_s, shift_s, ssm_s, conv_s = (jnp.stack(z) for z in zip(*new_s))
    return (h_p, h_s, gla_p, rwkv_p, shift_p, ssm_p, conv_p, gla_s, rwkv_s, shift_s, ssm_s, conv_s)
```

```python
import functools

import jax
import jax.numpy as jnp
from jax import lax
from jax.experimental import pallas as pl
from jax.experimental.pallas import tpu as pltpu

F32 = jnp.float32
BF16 = jnp.bfloat16

D_MODEL = 1024
CHUNK = 64
GLA_HEADS, GLA_DK, GLA_DV, GLA_RANK = 4, 128, 256, 16
GLA_GATE_NORM = 16.0
GLA_NORM_EPS = 1e-5
RWKV_HEAD, RWKV_HEADS, RWKV_DIM = 64, 16, 1024
RWKV_PROJ = 3 * RWKV_DIM + 64 + 64 + 128
RWKV_GN_EPS = 64e-5
SSM_INNER, SSM_HEADDIM, SSM_HEADS, SSM_GROUPS, SSM_STATE, SSM_CONV = 2048, 64, 32, 4, 128, 4
SSM_CONV_DIM = SSM_INNER + 2 * SSM_GROUPS * SSM_STATE
SSM_NORM_EPS = 1e-5
N_EXPERTS, TOP_K, N_EXPERT_GROUPS, TOPK_GROUPS, D_EXPERT = 64, 8, 8, 4, 256
ROUTED_SCALE = 2.5
DEPTH = 2
DN_ALPHA = (2.0 * DEPTH) ** 0.25
LN_EPS = 1e-5

OFF_GATE, OFF_GLA, OFF_RWKV, OFF_SSM = 0, 3072, 6144, 9216
OFF_RLOW, OFF_GALOW, OFF_DT = 14336, 14592, 14720
N_U = 15360
VMEM_LIMIT = 56 * 1024 * 1024


def _mm(a, b):
    return jnp.dot(a.astype(BF16), b.astype(BF16), preferred_element_type=F32)


def _mm_nt(a, b):
    return lax.dot_general(a.astype(BF16), b.astype(BF16), (((1,), (1,)), ((), ())),
                           preferred_element_type=F32)


def _mm_tn(a, b):
    return lax.dot_general(a.astype(BF16), b.astype(BF16), (((0,), (0,)), ((), ())),
                           preferred_element_type=F32)


def _sigmoid(x):
    return 1.0 / (1.0 + jnp.exp(-x))


def _silu(x):
    return x * _sigmoid(x)


def _softplus(x):
    return jnp.maximum(x, 0.0) + jnp.log(1.0 + jnp.exp(-jnp.abs(x)))


def _layer_norm(x, g, b):
    mu = jnp.mean(x, -1, keepdims=True)
    xc = x - mu
    var = jnp.mean(xc * xc, -1, keepdims=True)
    return xc * lax.rsqrt(var + LN_EPS) * g + b


def _tril_mask(n, strict=False):
    r = lax.broadcasted_iota(jnp.int32, (n, n), 0)
    c = lax.broadcasted_iota(jnp.int32, (n, n), 1)
    return (r > c) if strict else (r >= c)


def _cumsum_rows(x, tril_bf16):
    x1 = x.astype(BF16)
    r1 = x - x1.astype(F32)
    x2 = r1.astype(BF16)
    x3 = (r1 - x2.astype(F32)).astype(BF16)
    dot = lambda p: jnp.dot(tril_bf16, p, preferred_element_type=F32)
    return dot(x1) + dot(x2) + dot(x3)


def _ln_kernel(x_ref, g_ref, b_ref, o_ref):
    o_ref[...] = _layer_norm(x_ref[...], g_ref[...], b_ref[...])


def _ln_call(x, g, b):
    t = x.shape[0]
    tm = min(t, 512)
    return pl.pallas_call(
        _ln_kernel,
        name="ln_in",
        grid=(t // tm,),
        in_specs=[pl.BlockSpec((tm, D_MODEL), lambda i: (i, 0)),
                  pl.BlockSpec((1, D_MODEL), lambda i: (0, 0)),
                  pl.BlockSpec((1, D_MODEL), lambda i: (0, 0))],
        out_specs=pl.BlockSpec((tm, D_MODEL), lambda i: (i, 0)),
        out_shape=jax.ShapeDtypeStruct((t, D_MODEL), F32),
        compiler_params=pltpu.CompilerParams(dimension_semantics=("parallel",)),
    )(x, g.reshape(1, -1), b.reshape(1, -1))


def _inproj_kernel(x_ref, w_ref, o_ref, xb_ref):
    @pl.when(pl.program_id(1) == 0)
    def _():
        xb_ref[...] = x_ref[...].astype(BF16)

    o_ref[...] = jnp.dot(xb_ref[...], w_ref[...], preferred_element_type=F32)


def _inproj_call(h, w):
    t, n = h.shape[0], w.shape[1]
    tm = min(t, 1024)
    tn = 512
    return pl.pallas_call(
        _inproj_kernel,
        name="inproj",
        grid=(t // tm, n // tn),
        in_specs=[pl.BlockSpec((tm, D_MODEL), lambda i, j: (i, 0)),
                  pl.BlockSpec((D_MODEL, tn), lambda i, j: (0, j))],
        out_specs=pl.BlockSpec((tm, tn), lambda i, j: (i, j)),
        out_shape=jax.ShapeDtypeStruct((t, n), F32),
        scratch_shapes=[pltpu.VMEM((tm, D_MODEL), BF16)],
        compiler_params=pltpu.CompilerParams(dimension_semantics=("parallel", "arbitrary")),
    )(h, w)


def _gla_kernel(q_ref, k_ref, v_ref, r_ref, al_ref, wa2_ref, ba_ref, ng_ref, s0_ref,
                o_ref, sn_ref, s_ref, *, nc):
    i = pl.program_id(2)

    @pl.when(i == 0)
    def _():
        s_ref[...] = s0_ref[0, 0]

    tril = _tril_mask(CHUNK)
    tril_b = tril.astype(BF16)
    eye = (lax.broadcasted_iota(jnp.int32, (GLA_DK, GLA_DK), 0)
           == lax.broadcasted_iota(jnp.int32, (GLA_DK, GLA_DK), 1))
    for c in range(nc):
        sl = pl.ds(c * CHUNK, CHUNK)
        q, k, v = q_ref[sl, :], k_ref[sl, :], v_ref[sl, :]
        z = _mm(al_ref[sl, :], wa2_ref[...]) + ba_ref[...]
        la = -_softplus(-z) * (1.0 / GLA_GATE_NORM)
        g = _cumsum_rows(la, tril_b)
        g_last = g[CHUNK - 1:CHUNK, :]
        q_dec = q * jnp.exp(g) * (GLA_DK ** -0.5)
        k_inv = k * jnp.exp(-g)
        k_end = k * jnp.exp(g_last - g)
        scores = jnp.where(tril, _mm_nt(q_dec, k_inv), 0.0)
        s_prev = s_ref[...]
        o = _mm(scores, v) + _mm(q_dec, s_prev)
        dec_row = jnp.exp(g_last)
        dec_col = jnp.sum(jnp.where(eye, dec_row, 0.0), axis=1, keepdims=True)
        s_ref[...] = s_prev * dec_col + _mm_tn(k_end, v)
        o = o * lax.rsqrt(jnp.mean(o * o, -1, keepdims=True) + GLA_NORM_EPS) * ng_ref[...]
        o_ref[sl, :] = (o * _silu(r_ref[sl, :])).astype(o_ref.dtype)

    @pl.when(i == pl.num_programs(2) - 1)
    def _():
        sn_ref[0, 0] = s_ref[...]


def _gla_call(u, s0, wa2p, ba, ng, b, l):
    lb = min(l, 256)
    nb = l // lb
    row = lambda bb, h, i: bb * nb + i
    spec = lambda w, off: pl.BlockSpec((lb, w), lambda bb, h, i: (row(bb, h, i), off + h))
    return pl.pallas_call(
        functools.partial(_gla_kernel, nc=lb // CHUNK),
        name="gla_mixer",
        grid=(b, GLA_HEADS, nb),
        in_specs=[spec(128, OFF_GLA // 128), spec(128, (OFF_GLA + 512) // 128),
                  spec(256, (OFF_GLA + 1024) // 256), spec(256, (OFF_GLA + 2048) // 256),
                  pl.BlockSpec((lb, 128), lambda bb, h, i: (row(bb, h, i), OFF_GALOW // 128)),
                  pl.BlockSpec((128, 128), lambda bb, h, i: (0, h)),
                  pl.BlockSpec((1, 128), lambda bb, h, i: (0, h)),
                  pl.BlockSpec((1, 256), lambda bb, h, i: (0, 0)),
                  pl.BlockSpec((1, 1, GLA_DK, GLA_DV), lambda bb, h, i: (bb, h, 0, 0))],
        out_specs=[pl.BlockSpec((lb, 256), lambda bb, h, i: (row(bb, h, i), h)),
                   pl.BlockSpec((1, 1, GLA_DK, GLA_DV), lambda bb, h, i: (bb, h, 0, 0))],
        out_shape=[jax.ShapeDtypeStruct((b * l, GLA_HEADS * GLA_DV), BF16),
                   jax.ShapeDtypeStruct((b, GLA_HEADS, GLA_DK, GLA_DV), F32)],
        scratch_shapes=[pltpu.VMEM((GLA_DK, GLA_DV), F32)],
        compiler_params=pltpu.CompilerParams(dimension_semantics=("parallel", "parallel", "arbitrary")),
    )(u, u, u, u, u, wa2p, ba, ng, s0)


def _half_sum(x, lo):
    s0 = jnp.sum(jnp.where(lo, x, 0.0), -1, keepdims=True)
    s1 = jnp.sum(jnp.where(lo, 0.0, x), -1, keepdims=True)
    return jnp.where(lo, s0, s1)


def _unit_lower_inverse(n, eye_f):
    x = eye_f + n
    p = n
    for _ in range(5):
        p = _mm(p, p)
        x = x + _mm(x, p)
    return x


def _rwkv_kernel(r_ref, k_ref, v_ref, low_ref, shr_ref, shk_ref, shv_ref, shl_ref,
                 mur_ref, muk_ref, muv_ref, mul_ref, w0_ref, w2_ref, a0_ref, a2_ref, g2_ref,
                 kk_ref, ka_ref, rk_ref, lng_ref, lnb_ref, s0_ref,
                 o_ref, sn_ref, s_ref, sh_ref, *, nc):
    i = pl.program_id(2)
    lb = nc * CHUNK
    hd = RWKV_HEAD

    @pl.when(i == 0)
    def _():
        s_ref[...] = s0_ref[0]
        sh_ref[7:8, 0:128] = shr_ref[0]
        sh_ref[7:8, 128:256] = shk_ref[0]
        sh_ref[7:8, 256:384] = shv_ref[0]
        sh_ref[7:8, 384:640] = shl_ref[0]

    sh_ref[8:8 + lb, 0:128] = r_ref[...]
    sh_ref[8:8 + lb, 128:256] = k_ref[...]
    sh_ref[8:8 + lb, 256:384] = v_ref[...]
    sh_ref[8:8 + lb, 384:640] = low_ref[...]

    tril = _tril_mask(CHUNK)
    stril = _tril_mask(CHUNK, strict=True)
    tril_b = tril.astype(BF16)
    eye_f = (lax.broadcasted_iota(jnp.int32, (CHUNK, CHUNK), 0)
             == lax.broadcasted_iota(jnp.int32, (CHUNK, CHUNK), 1)).astype(F32)
    lo = lax.broadcasted_iota(jnp.int32, (1, 128), 1) < hd

    for c in range(nc):
        cur = sh_ref[8 + c * CHUNK:8 + (c + 1) * CHUNK, :]
        prev = sh_ref[7 + c * CHUNK:7 + (c + 1) * CHUNK, :]
        shift = lambda a, b, mu: a + (b - a) * mu
        r = shift(cur[:, 0:128], prev[:, 0:128], mur_ref[...])
        k = shift(cur[:, 128:256], prev[:, 128:256], muk_ref[...])
        v = shift(cur[:, 256:384], prev[:, 256:384], muv_ref[...])
        low = shift(cur[:, 384:640], prev[:, 384:640], mul_ref[...])
        zw = w0_ref[...] + _mm(jnp.tanh(low[:, 0:64]), w2_ref[...])
        lw = -jnp.exp(-_softplus(-zw) - 0.5)
        a = _sigmoid(a0_ref[...] + _mm(low[:, 64:128], a2_ref[...]))
        gate = _mm(_sigmoid(low[:, 128:256]), g2_ref[...])
        kk = k * kk_ref[...]
        kk = kk * lax.rsqrt(jnp.maximum(_half_sum(kk * kk, lo), 1e-24))
        k2 = k * (1.0 + (a - 1.0) * ka_ref[...])
        lg = _cumsum_rows(lw, tril_b)
        e_neg = jnp.exp(-lg)
        a_t = -kk * jnp.exp(lg - lw)
        b_t = kk * a * e_neg
        k_t = k2 * e_neg
        r_t = r * jnp.exp(lg)
        gam_end = jnp.exp(lg[CHUNK - 1:CHUNK, :])
        ys = []
        for hh in range(2):
            hs = slice(hh * hd, (hh + 1) * hd)
            ah, bh, kh, rh, vh = a_t[:, hs], b_t[:, hs], k_t[:, hs], r_t[:, hs], v[:, hs]
            s_prev = s_ref[hh]
            n_ab = jnp.where(stril, _mm_nt(ah, bh), 0.0)
            n_ak = jnp.where(stril, _mm_nt(ah, kh), 0.0)
            m_rb = jnp.where(tril, _mm_nt(rh, bh), 0.0)
            m_rk = jnp.where(tril, _mm_nt(rh, kh), 0.0)
            t_inv = _unit_lower_inverse(n_ab, eye_f)
            w = _mm_nt(ah, s_prev) + _mm(n_ak, vh)
            uu = _mm(t_inv, w)
            ys.append(_mm_nt(rh, s_prev) + _mm(m_rb, uu) + _mm(m_rk, vh))
            s_ref[hh] = (s_prev + _mm_tn(uu, bh) + _mm_tn(vh, kh)) * gam_end[:, hs]
        y = jnp.concatenate(ys, axis=1)
        mean = _half_sum(y, lo) * (1.0 / hd)
        yc = y - mean
        var = _half_sum(yc * yc, lo) * (1.0 / hd)
        y = yc * lax.rsqrt(var + RWKV_GN_EPS) * lng_ref[...] + lnb_ref[...]
        bonus = _half_sum(r * k2 * rk_ref[...], lo) * v
        o_ref[c * CHUNK:(c + 1) * CHUNK, :] = ((y + bonus) * gate).astype(o_ref.dtype)

    sh_ref[7:8, :] = sh_ref[7 + lb:8 + lb, :]

    @pl.when(i == pl.num_programs(2) - 1)
    def _():
        sn_ref[0] = s_ref[...]


def _rwkv_call(u, shift_buf, s0, p, b, l):
    lb = min(l, 256)
    nb = l // lb
    row = lambda bb, h, i: bb * nb + i
    ublk = lambda off: pl.BlockSpec((lb, 128), lambda bb, h, i: (row(bb, h, i), off // 128 + h))
    sblk = lambda off: pl.BlockSpec((1, 1, 128), lambda bb, h, i: (bb, 0, off // 128 + h))
    vec = lambda: pl.BlockSpec((1, 128), lambda bb, h, i: (0, h))
    const = lambda shape: pl.BlockSpec(shape, lambda bb, h, i: (0,) * len(shape))
    return pl.pallas_call(
        functools.partial(_rwkv_kernel, nc=lb // CHUNK),
        name="rwkv_mixer",
        grid=(b, RWKV_HEADS // 2, nb),
        in_specs=[ublk(OFF_RWKV), ublk(OFF_RWKV + 1024), ublk(OFF_RWKV + 2048),
                  pl.BlockSpec((lb, 256), lambda bb, h, i: (row(bb, h, i), OFF_RLOW // 256)),
                  sblk(0), sblk(1024), sblk(2048),
                  pl.BlockSpec((1, 1, 256), lambda bb, h, i: (bb, 0, 3072 // 256)),
                  vec(), pl.BlockSpec((1, 128), lambda bb, h, i: (0, 8 + h)),
                  pl.BlockSpec((1, 128), lambda bb, h, i: (0, 16 + h)),
                  pl.BlockSpec((1, 256), lambda bb, h, i: (0, 3072 // 256)),
                  vec(), pl.BlockSpec((64, 128), lambda bb, h, i: (0, h)),
                  vec(), pl.BlockSpec((64, 128), lambda bb, h, i: (0, h)),
                  pl.BlockSpec((128, 128), lambda bb, h, i: (0, h)),
                  vec(), vec(), vec(), vec(), vec(),
                  pl.BlockSpec((1, 2, RWKV_HEAD, RWKV_HEAD), lambda bb, h, i: (bb, h, 0, 0))],
        out_specs=[pl.BlockSpec((lb, 128), lambda bb, h, i: (row(bb, h, i), h)),
                   pl.BlockSpec((1, 2, RWKV_HEAD, RWKV_HEAD), lambda bb, h, i: (bb, h, 0, 0))],
        out_shape=[jax.ShapeDtypeStruct((b * l, RWKV_DIM), BF16),
                   jax.ShapeDtypeStruct((b, RWKV_HEADS, RWKV_HEAD, RWKV_HEAD), F32)],
        scratch_shapes=[pltpu.VMEM((2, RWKV_HEAD, RWKV_HEAD), F32),
                        pltpu.VMEM((lb + 8, 640), F32)],
        compiler_params=pltpu.CompilerParams(dimension_semantics=("parallel", "parallel", "arbitrary")),
    )(u, u, u, u, shift_buf, shift_buf, shift_buf, shift_buf,
      p["mu"], p["mu"], p["mu"], p["mu"], p["w0"], p["w2"], p["a0"], p["a2"], p["g2"],
      p["k_k"], p["k_a"], p["r_k"], p["ln_g"], p["ln_b"], s0)


def _ssd_kernel(z_ref, x_ref, bm_ref, cm_ref, dt_ref, cbx_ref, cbb_ref, cbc_ref,
                cwx_ref, cwb_ref, cwc_ref, cbiasx_ref, cbiasb_ref, cbiasc_ref,
                dtb_ref, alog_ref, dsk_ref, ng_ref, s0_ref,
                o_ref, sn_ref, s_ref, full_ref, y_ref, *, nc):
    i = pl.program_id(2)
    lb = nc * CHUNK
    hpg = SSM_HEADS // SSM_GROUPS
    p = SSM_HEADDIM
    wx = hpg * p

    @pl.when(i == 0)
    def _():
        s_ref[...] = s0_ref[0].reshape(hpg * p, SSM_STATE)
        full_ref[5:8, 0:wx] = cbx_ref[0]
        full_ref[5:8, wx:wx + 128] = cbb_ref[0]
        full_ref[5:8, wx + 128:wx + 256] = cbc_ref[0]

    rnd = lambda a: a.astype(BF16).astype(F32)

    @pl.when(i == 0)
    def _():
        full_ref[5:8, :] = rnd(full_ref[5:8, :])

    full_ref[8:8 + lb, 0:wx] = rnd(x_ref[...])
    full_ref[8:8 + lb, wx:wx + 128] = rnd(bm_ref[...])
    full_ref[8:8 + lb, wx + 128:wx + 256] = rnd(cm_ref[...])

    tril = _tril_mask(CHUNK)
    tril_b = tril.astype(BF16)
    cw = jnp.concatenate([cwx_ref[...], cwb_ref[...], cwc_ref[...]], axis=1)
    cbias = jnp.concatenate([cbiasx_ref[...], cbiasb_ref[...], cbiasc_ref[...]], axis=1)
    a_row = -jnp.exp(alog_ref[0])

    for c in range(nc):
        base = c * CHUNK
        conv = cbias
        for j in range(SSM_CONV):
            conv = conv + cw[j:j + 1, :] * full_ref[5 + j + base:5 + j + base + CHUNK, :]
        conv = _silu(conv)
        xs, bm, cm = conv[:, 0:wx], conv[:, wx:wx + 128], conv[:, wx + 128:wx + 256]
        dtv = _softplus(dt_ref[base:base + CHUNK, :] + dtb_ref[0])
        acum = _cumsum_rows(dtv * a_row, tril_b)
        acum_t = jnp.transpose(acum)
        cb = _mm_nt(cm, bm)
        for j in range(hpg):
            col = acum[:, j:j + 1]
            rowv = acum_t[j:j + 1, :]
            a_last = acum[CHUNK - 1:CHUNK, j:j + 1]
            seg = col - rowv
            dec = jnp.where(tril, jnp.exp(jnp.where(tril, seg, 0.0)), 0.0)
            xj = xs[:, j * p:(j + 1) * p]
            xdt = xj * dtv[:, j:j + 1]
            s_prev = s_ref[j * p:(j + 1) * p, :]
            y = _mm(cb * dec, xdt)
            y = y + _mm_nt(cm, s_prev) * jnp.exp(col)
            s_ref[j * p:(j + 1) * p, :] = (s_prev * jnp.exp(a_last)
                                           + _mm_tn(xdt * jnp.exp(a_last - col), bm))
            y_ref[:, j * p:(j + 1) * p] = y
        y = (y_ref[...] + xs * dsk_ref[...]) * _silu(z_ref[base:base + CHUNK, :])
        y = y * lax.rsqrt(jnp.mean(y * y, -1, keepdims=True) + SSM_NORM_EPS) * ng_ref[...]
        o_ref[base:base + CHUNK, :] = y.astype(o_ref.dtype)

    full_ref[5:8, :] = full_ref[5 + lb:8 + lb, :]

    @pl.when(i == pl.num_programs(2) - 1)
    def _():
        sn_ref[0] = s_ref[...].reshape(hpg, p, SSM_STATE)


def _ssd_call(u, conv_buf, s0, p, b, l):
    lb = min(l, 256)
    nb = l // lb
    hpg = SSM_HEADS // SSM_GROUPS
    row = lambda bb, g, i: bb * nb + i
    ublk = lambda w, off: pl.BlockSpec((lb, w), lambda bb, g, i: (row(bb, g, i), off // w + g))
    gvec = lambda w, off: pl.BlockSpec((1, w), lambda bb, g, i: (0, off // w + g))
    gtap = lambda w, off: pl.BlockSpec((SSM_CONV, w), lambda bb, g, i: (0, off // w + g))
    gbuf = lambda w, off: pl.BlockSpec((1, SSM_CONV - 1, w), lambda bb, g, i: (bb, 0, off // w + g))
    return pl.pallas_call(
        functools.partial(_ssd_kernel, nc=lb // CHUNK),
        name="ssd_mixer",
        grid=(b, SSM_GROUPS, nb),
        in_specs=[ublk(512, OFF_SSM), ublk(512, OFF_SSM + 2048),
                  ublk(128, OFF_SSM + 4096), ublk(128, OFF_SSM + 4608), ublk(128, OFF_DT),
                  gbuf(512, 0), gbuf(128, 2048), gbuf(128, 2560),
                  gtap(512, 0), gtap(128, 2048), gtap(128, 2560),
                  gvec(512, 0), gvec(128, 2048), gvec(128, 2560),
                  pl.BlockSpec((1, 1, 128), lambda bb, g, i: (g, 0, 0)),
                  pl.BlockSpec((1, 1, 128), lambda bb, g, i: (g, 0, 0)),
                  gvec(512, 0), gvec(512, 0),
                  pl.BlockSpec((1, hpg, SSM_HEADDIM, SSM_STATE), lambda bb, g, i: (bb, g, 0, 0))],
        out_specs=[pl.BlockSpec((lb, 512), lambda bb, g, i: (row(bb, g, i), g)),
                   pl.BlockSpec((1, hpg, SSM_HEADDIM, SSM_STATE), lambda bb, g, i: (bb, g, 0, 0))],
        out_shape=[jax.ShapeDtypeStruct((b * l, SSM_INNER), BF16),
                   jax.ShapeDtypeStruct((b, SSM_HEADS, SSM_HEADDIM, SSM_STATE), F32)],
        scratch_shapes=[pltpu.VMEM((hpg * SSM_HEADDIM, SSM_STATE), F32),
                        pltpu.VMEM((lb + 8, 768), F32),
                        pltpu.VMEM((CHUNK, 512), F32)],
        compiler_params=pltpu.CompilerParams(dimension_semantics=("parallel", "parallel", "arbitrary")),
    )(u, u, u, u, u, conv_buf, conv_buf, conv_buf,
      p["conv_w"], p["conv_w"], p["conv_w"], p["conv_b"], p["conv_b"], p["conv_b"],
      p["dt_bias"], p["a_log"], p["d_skip"], p["norm_g"], s0)


def _merge_kernel(gate_ref, og_ref, or_ref, os_ref, x_ref, wg_ref, wr_ref, ws_ref, wo_ref,
                  g_ref, b_ref, o_ref):
    d = D_MODEL
    m = _sigmoid(gate_ref[:, 0:d]) * jnp.dot(og_ref[...], wg_ref[...], preferred_element_type=F32)
    m = m + _sigmoid(gate_ref[:, d:2 * d]) * jnp.dot(or_ref[...], wr_ref[...], preferred_element_type=F32)
    m = m + _sigmoid(gate_ref[:, 2 * d:3 * d]) * jnp.dot(os_ref[...], ws_ref[...], preferred_element_type=F32)
    y = DN_ALPHA * x_ref[...] + _mm(m, wo_ref[...])
    o_ref[...] = _layer_norm(y, g_ref[...], b_ref[...])


def _merge_call(u, og, orr, os_, x, wg, wr, ws, wo, g, b):
    t = x.shape[0]
    tm = min(t, 256)
    tok = lambda w: pl.BlockSpec((tm, w), lambda i: (i, 0))
    const = lambda r, c: pl.BlockSpec((r, c), lambda i: (0, 0))
    return pl.pallas_call(
        _merge_kernel,
        name="merge_outproj_ln",
        grid=(t // tm,),
        in_specs=[tok(3 * D_MODEL), tok(D_MODEL), tok(D_MODEL), tok(SSM_INNER), tok(D_MODEL),
                  const(D_MODEL, D_MODEL), const(D_MODEL, D_MODEL), const(SSM_INNER, D_MODEL),
                  const(D_MODEL, D_MODEL), const(1, D_MODEL), const(1, D_MODEL)],
        out_specs=tok(D_MODEL),
        out_shape=jax.ShapeDtypeStruct((t, D_MODEL), F32),
        compiler_params=pltpu.CompilerParams(dimension_semantics=("parallel",),
                                             vmem_limit_bytes=VMEM_LIMIT),
    )(u, og, orr, os_, x, wg, wr, ws, wo, g, b)


def _router_gates(xb, wrt_ref, rb_ref, key_ref):
    tm = xb.shape[0]
    per_group = N_EXPERTS // N_EXPERT_GROUPS
    s_t = _sigmoid(_mm_nt(wrt_ref[...], xb))
    sb = s_t + rb_ref[...]
    gscore = []
    for g in range(N_EXPERT_GROUPS):
        xg = sb[g * per_group:(g + 1) * per_group, :]
        m1 = jnp.max(xg, axis=0, keepdims=True)
        eq = xg == m1
        cnt = jnp.sum(eq.astype(F32), axis=0, keepdims=True)
        m2 = jnp.max(jnp.where(eq, -jnp.inf, xg), axis=0, keepdims=True)
        gscore.append(m1 + jnp.where(cnt >= 2.0, m1, m2))
    for g in range(N_EXPERT_GROUPS):
        rank = jnp.zeros((1, tm), F32)
        for g2 in range(N_EXPERT_GROUPS):
            if g2 == g:
                continue
            beats = (gscore[g2] >= gscore[g]) if g2 < g else (gscore[g2] > gscore[g])
            rank = rank + beats.astype(F32)
        keep = rank < float(TOPK_GROUPS)
        key_ref[g * per_group:(g + 1) * per_group, :] = jnp.where(
            keep, sb[g * per_group:(g + 1) * per_group, :], -jnp.inf)
    key = key_ref[...]
    eidx = lax.broadcasted_iota(jnp.int32, (N_EXPERTS, tm), 0)

    def body(ep, rank):
        rowv = key_ref[pl.ds(ep, 1), :]
        beats = (rowv > key) | ((rowv == key) & (ep < eidx))
        return rank + beats.astype(F32)

    rank = lax.fori_loop(0, N_EXPERTS, body, jnp.zeros((N_EXPERTS, tm), F32))
    w = jnp.where(rank < float(TOP_K), s_t, 0.0)
    return w / jnp.sum(w, axis=0, keepdims=True) * ROUTED_SCALE


def _moe_kernel(x_ref, wrt_ref, rb_ref, w1_ref, w3_ref, w2_ref, s1_ref, s3_ref, s2_ref,
                g_ref, b_ref, o_ref, xb_ref, gates_ref, key_ref, *, sub):
    e = pl.program_id(1)
    tm = x_ref.shape[0]

    @pl.when(e == 0)
    def _():
        xb_ref[...] = x_ref[...].astype(BF16)
        g_t = _router_gates(xb_ref[...], wrt_ref, rb_ref, key_ref)
        gates_ref[...] = jnp.transpose(jnp.concatenate([g_t, jnp.zeros_like(g_t)], axis=0))
        for s in range(tm // sub):
            rows = pl.ds(s * sub, sub)
            xs = xb_ref[rows, :]
            hid = _silu(jnp.dot(xs, s1_ref[...], preferred_element_type=F32)) * jnp.dot(
                xs, s3_ref[...], preferred_element_type=F32)
            o_ref[rows, :] = _mm(hid, s2_ref[...])

    lane = lax.broadcasted_iota(jnp.int32, (1, 128), 1)
    for s in range(tm // sub):
        rows = pl.ds(s * sub, sub)
        xs = xb_ref[rows, :]
        gcol = jnp.sum(jnp.where(lane == e, gates_ref[rows, :], 0.0), axis=1, keepdims=True)
        hid = _silu(jnp.dot(xs, w1_ref[0], preferred_element_type=F32)) * jnp.dot(
            xs, w3_ref[0], preferred_element_type=F32)
        o_ref[rows, :] += _mm(hid * gcol, w2_ref[0])

    @pl.when(e == pl.num_programs(1) - 1)
    def _():
        o_ref[...] = _layer_norm(DN_ALPHA * x_ref[...] + o_ref[...], g_ref[...], b_ref[...])


def _moe_call(h, wrt, rb, w1, w3, w2, s1, s3, s2, g, b):
    t = h.shape[0]
    tm = min(t, 1024)
    sub = min(tm, 256)
    const = lambda r, c: pl.BlockSpec((r, c), lambda i, e: (0, 0))
    return pl.pallas_call(
        functools.partial(_moe_kernel, sub=sub),
        name="moe_ln",
        grid=(t // tm, N_EXPERTS),
        in_specs=[pl.BlockSpec((tm, D_MODEL), lambda i, e: (i, 0)),
                  const(N_EXPERTS, D_MODEL), const(N_EXPERTS, 1),
                  pl.BlockSpec((1, D_MODEL, D_EXPERT), lambda i, e: (e, 0, 0)),
                  pl.BlockSpec((1, D_MODEL, D_EXPERT), lambda i, e: (e, 0, 0)),
                  pl.BlockSpec((1, D_EXPERT, D_MODEL), lambda i, e: (e, 0, 0)),
                  const(D_MODEL, D_EXPERT), const(D_MODEL, D_EXPERT), const(D_EXPERT, D_MODEL),
                  const(1, D_MODEL), const(1, D_MODEL)],
        out_specs=pl.BlockSpec((tm, D_MODEL), lambda i, e: (i, 0)),
        out_shape=jax.ShapeDtypeStruct((t, D_MODEL), F32),
        scratch_shapes=[pltpu.VMEM((tm, D_MODEL), BF16),
                        pltpu.VMEM((tm, 128), F32),
                        pltpu.VMEM((N_EXPERTS, tm), F32)],
        compiler_params=pltpu.CompilerParams(dimension_semantics=("parallel", "arbitrary"),
                                             vmem_limit_bytes=VMEM_LIMIT),
    )(h, wrt, rb, w1, w3, w2, s1, s3, s2, g, b)


def _prep_layer(w_in, gla_wa2, gla_ba, gla_norm_g, gla_wo,
                rwkv_mu, rwkv_w0, rwkv_w2, rwkv_a0, rwkv_a2, rwkv_g2, rwkv_k_k, rwkv_k_a, rwkv_r_k,
                rwkv_ln_g, rwkv_ln_b, rwkv_wo,
                ssm_conv_w, ssm_conv_b, ssm_dt_bias, ssm_a_log, ssm_d, ssm_norm_g, ssm_wo,
                w_out, ln1_g, ln1_b,
                w_router, router_bias, exp_w1, exp_w3, exp_w2, sh_w1, sh_w3, sh_w2, ln2_g, ln2_b):
    d = D_MODEL
    o_gla, o_rwkv = 3 * d, 3 * d + 3088
    o_ssm = o_rwkv + RWKV_PROJ
    zeros = lambda n: jnp.zeros((d, n), w_in.dtype)
    dt0 = o_ssm + SSM_INNER + SSM_CONV_DIM
    hpg = SSM_HEADS // SSM_GROUPS
    dt_cols = []
    for g in range(SSM_GROUPS):
        dt_cols += [w_in[:, dt0 + g * hpg:dt0 + (g + 1) * hpg], zeros(128 - hpg)]
    w_u = jnp.concatenate(
        [w_in[:, 0:3 * d], w_in[:, o_gla:o_gla + 3072], w_in[:, o_rwkv:o_rwkv + 3072],
         w_in[:, o_ssm:o_ssm + SSM_INNER + SSM_CONV_DIM], w_in[:, o_rwkv + 3072:o_rwkv + 3328],
         w_in[:, o_gla + 3072:o_gla + 3088], zeros(128 - GLA_RANK)] + dt_cols + [zeros(128)],
        axis=1).astype(BF16)
    assert w_u.shape[1] == N_U
    row = lambda v: v.reshape(1, -1)
    pad_groups = lambda v: jnp.pad(v.reshape(SSM_GROUPS, 1, hpg), ((0, 0), (0, 0), (0, 128 - hpg)))
    return dict(
        w_u=w_u,
        gla=dict(wa2=jnp.pad(gla_wa2, ((0, 128 - GLA_RANK), (0, 0))).astype(BF16),
                 ba=row(gla_ba), ng=row(gla_norm_g)),
        rwkv=dict(mu=row(rwkv_mu), w0=row(rwkv_w0), w2=rwkv_w2.astype(BF16), a0=row(rwkv_a0),
                  a2=rwkv_a2.astype(BF16), g2=rwkv_g2.astype(BF16), k_k=row(rwkv_k_k),
                  k_a=row(rwkv_k_a), r_k=row(rwkv_r_k), ln_g=row(rwkv_ln_g), ln_b=row(rwkv_ln_b)),
        ssm=dict(conv_w=ssm_conv_w, conv_b=row(ssm_conv_b), dt_bias=pad_groups(ssm_dt_bias),
                 a_log=pad_groups(ssm_a_log), d_skip=row(jnp.repeat(ssm_d, SSM_HEADDIM)),
                 norm_g=row(ssm_norm_g)),
        merge=(gla_wo.astype(BF16), rwkv_wo.astype(BF16), ssm_wo.astype(BF16), w_out.astype(BF16),
               row(ln1_g), row(ln1_b)),
        moe=(jnp.transpose(w_router).astype(BF16), router_bias.reshape(-1, 1),
             exp_w1.astype(BF16), exp_w3.astype(BF16), exp_w2.astype(BF16),
             sh_w1.astype(BF16), sh_w3.astype(BF16), sh_w2.astype(BF16), row(ln2_g), row(ln2_b)),
    )


def _layer(h, st_gla, st_rwkv, st_shift, st_ssm, st_conv, lp, b, l):
    u = _inproj_call(h, lp["w_u"])
    o_gla, s_gla = _gla_call(u, st_gla, lp["gla"]["wa2"], lp["gla"]["ba"], lp["gla"]["ng"], b, l)
    o_rwkv, s_rwkv = _rwkv_call(u, st_shift, st_rwkv, lp["rwkv"], b, l)
    o_ssm, s_ssm = _ssd_call(u, st_conv, st_ssm, lp["ssm"], b, l)
    h1 = _merge_call(u, o_gla, o_rwkv, o_ssm, h, *lp["merge"])
    h2 = _moe_call(h1, *lp["moe"])
    u3 = u.reshape(b, l, N_U)
    shift_new = jnp.concatenate([u3[:, l - 1:, OFF_RWKV:OFF_RWKV + 3072],
                                 u3[:, l - 1:, OFF_RLOW:OFF_RLOW + 256]], axis=-1)
    conv_new = u3[:, l - (SSM_CONV - 1):, OFF_SSM + SSM_INNER:OFF_SSM + SSM_INNER + SSM_CONV_DIM]
    return h2, s_gla, s_rwkv, shift_new, s_ssm, conv_new


def _zero_states(batch):
    return (jnp.zeros((batch, GLA_HEADS, GLA_DK, GLA_DV), F32),
            jnp.zeros((batch, RWKV_HEADS, RWKV_HEAD, RWKV_HEAD), F32),
            jnp.zeros((batch, 1, RWKV_PROJ), F32),
            jnp.zeros((batch, SSM_HEADS, SSM_HEADDIM, SSM_STATE), F32),
            jnp.zeros((batch, SSM_CONV - 1, SSM_CONV_DIM), F32))


def kernel(x_prompt, x_sample, state_gla, state_rwkv, state_rwkv_shift, state_ssm, state_ssm_conv,
           ln_in_g, ln_in_b, w_in, gla_wa2, gla_ba, gla_norm_g, gla_wo,
           rwkv_mu, rwkv_w0, rwkv_w2, rwkv_a0, rwkv_a2, rwkv_g2, rwkv_k_k, rwkv_k_a, rwkv_r_k,
           rwkv_ln_g, rwkv_ln_b, rwkv_wo,
           ssm_conv_w, ssm_conv_b, ssm_dt_bias, ssm_a_log, ssm_d, ssm_norm_g, ssm_wo,
           w_out, ln1_g, ln1_b,
           w_router, router_bias, exp_w1, exp_w3, exp_w2, sh_w1, sh_w3, sh_w2, ln2_g, ln2_b):
    layer_params = (w_in, gla_wa2, gla_ba, gla_norm_g, gla_wo,
                    rwkv_mu, rwkv_w0, rwkv_w2, rwkv_a0, rwkv_a2, rwkv_g2, rwkv_k_k, rwkv_k_a, rwkv_r_k,
                    rwkv_ln_g, rwkv_ln_b, rwkv_wo,
                    ssm_conv_w, ssm_conv_b, ssm_dt_bias, ssm_a_log, ssm_d, ssm_norm_g, ssm_wo,
                    w_out, ln1_g, ln1_b,
                    w_router, router_bias, exp_w1, exp_w3, exp_w2, sh_w1, sh_w3, sh_w2, ln2_g, ln2_b)
    depth = w_in.shape[0]
    bp, lp_, d = x_prompt.shape
    bs, ls, _ = x_sample.shape
    h_p = _ln_call(x_prompt.reshape(bp * lp_, d), ln_in_g, ln_in_b)
    h_s = _ln_call(x_sample.reshape(bs * ls, d), ln_in_g, ln_in_b)
    new_p, new_s = [], []
    for i in range(depth):
        lp = _prep_layer(*[t[i] for t in layer_params])
        h_p, *st_p = _layer(h_p, *_zero_states(bp), lp, bp, lp_)
        new_p.append(st_p)
        h_s, *st_s = _layer(h_s, state_gla[i], state_rwkv[i], state_rwkv_shift[i], state_ssm[i],
                            state_ssm_conv[i], lp, bs, ls)
        new_s.append(st_s)
    stack = lambda rows: tuple(jnp.stack(z) for z in zip(*rows))
    return (h_p.reshape(bp, lp_, d), h_s.reshape(bs, ls, d)) + stack(new_p) + stack(new_s)
```

```python
import functools

import jax
import jax.numpy as jnp
from jax import lax
from jax.experimental import pallas as pl
from jax.experimental.pallas import tpu as pltpu

F32 = jnp.float32
BF16 = jnp.bfloat16

D_MODEL = 1024
CHUNK = 64
GLA_HEADS, GLA_DK, GLA_DV, GLA_RANK = 4, 128, 256, 16
GLA_GATE_NORM = 16.0
GLA_NORM_EPS = 1e-5
RWKV_HEAD, RWKV_HEADS, RWKV_DIM = 64, 16, 1024
RWKV_PROJ = 3 * RWKV_DIM + 64 + 64 + 128
RWKV_GN_EPS = 64e-5
SSM_INNER, SSM_HEADDIM, SSM_HEADS, SSM_GROUPS, SSM_STATE, SSM_CONV = 2048, 64, 32, 4, 128, 4
SSM_CONV_DIM = SSM_INNER + 2 * SSM_GROUPS * SSM_STATE
SSM_NORM_EPS = 1e-5
N_EXPERTS, TOP_K, N_EXPERT_GROUPS, TOPK_GROUPS, D_EXPERT = 64, 8, 8, 4, 256
ROUTED_SCALE = 2.5
DEPTH = 2
DN_ALPHA = (2.0 * DEPTH) ** 0.25
LN_EPS = 1e-5

OFF_GATE, OFF_GLA, OFF_RWKV, OFF_SSM = 0, 3072, 6144, 9216
OFF_RLOW, OFF_GALOW, OFF_DT = 14336, 14592, 14720
N_U = 15360
VMEM_LIMIT = 56 * 1024 * 1024


def _mm(a, b):
    return jnp.dot(a.astype(BF16), b.astype(BF16), preferred_element_type=F32)


def _mm_nt(a, b):
    return lax.dot_general(a.astype(BF16), b.astype(BF16), (((1,), (1,)), ((), ())),
                           preferred_element_type=F32)


def _mm_tn(a, b):
    return lax.dot_general(a.astype(BF16), b.astype(BF16), (((0,), (0,)), ((), ())),
                           preferred_element_type=F32)


def _sigmoid(x):
    return 1.0 / (1.0 + jnp.exp(-x))


def _silu(x):
    return x * _sigmoid(x)


def _softplus(x):
    return jnp.maximum(x, 0.0) + jnp.log(1.0 + jnp.exp(-jnp.abs(x)))


def _layer_norm(x, g, b):
    mu = jnp.mean(x, -1, keepdims=True)
    xc = x - mu
    var = jnp.mean(xc * xc, -1, keepdims=True)
    return xc * lax.rsqrt(var + LN_EPS) * g + b


def _tril_mask(n, strict=False):
    r = lax.broadcasted_iota(jnp.int32, (n, n), 0)
    c = lax.broadcasted_iota(jnp.int32, (n, n), 1)
    return (r > c) if strict else (r >= c)


def _cumsum_rows(x, tril_bf16):
    x1 = x.astype(BF16)
    r1 = x - x1.astype(F32)
    x2 = r1.astype(BF16)
    x3 = (r1 - x2.astype(F32)).astype(BF16)
    dot = lambda p: jnp.dot(tril_bf16, p, preferred_element_type=F32)
    return dot(x1) + dot(x2) + dot(x3)


def _ln_kernel(x_ref, g_ref, b_ref, o_ref):
    o_ref[...] = _layer_norm(x_ref[...], g_ref[...], b_ref[...])


def _ln_call(x, g, b):
    t = x.shape[0]
    tm = min(t, 512)
    return pl.pallas_call(
        _ln_kernel,
        name="ln_in",
        grid=(t // tm,),
        in_specs=[pl.BlockSpec((tm, D_MODEL), lambda i: (i, 0)),
                  pl.BlockSpec((1, D_MODEL), lambda i: (0, 0)),
                  pl.BlockSpec((1, D_MODEL), lambda i: (0, 0))],
        out_specs=pl.BlockSpec((tm, D_MODEL), lambda i: (i, 0)),
        out_shape=jax.ShapeDtypeStruct((t, D_MODEL), F32),
        compiler_params=pltpu.CompilerParams(dimension_semantics=("parallel",)),
    )(x, g.reshape(1, -1), b.reshape(1, -1))


def _inproj_kernel(x_ref, w_ref, o_ref, xb_ref):
    @pl.when(pl.program_id(1) == 0)
    def _():
        xb_ref[...] = x_ref[...].astype(BF16)

    o_ref[...] = jnp.dot(xb_ref[...], w_ref[...], preferred_element_type=F32)


def _inproj_call(h, w):
    t, n = h.shape[0], w.shape[1]
    tm = min(t, 1024)
    tn = 1024
    return pl.pallas_call(
        _inproj_kernel,
        name="inproj",
        grid=(t // tm, n // tn),
        in_specs=[pl.BlockSpec((tm, D_MODEL), lambda i, j: (i, 0)),
                  pl.BlockSpec((D_MODEL, tn), lambda i, j: (0, j))],
        out_specs=pl.BlockSpec((tm, tn), lambda i, j: (i, j)),
        out_shape=jax.ShapeDtypeStruct((t, n), F32),
        scratch_shapes=[pltpu.VMEM((tm, D_MODEL), BF16)],
        compiler_params=pltpu.CompilerParams(dimension_semantics=("parallel", "arbitrary")),
    )(h, w)


def _gla_kernel(q_ref, k_ref, v_ref, r_ref, al_ref, wa2_ref, ba_ref, ng_ref, s0_ref,
                o_ref, sn_ref, s_ref, *, nc):
    i = pl.program_id(2)

    @pl.when(i == 0)
    def _():
        s_ref[...] = s0_ref[0, 0]

    tril = _tril_mask(CHUNK)
    tril_b = tril.astype(BF16)
    eye = (lax.broadcasted_iota(jnp.int32, (GLA_DK, GLA_DK), 0)
           == lax.broadcasted_iota(jnp.int32, (GLA_DK, GLA_DK), 1))
    chunks = range(nc)
    sl = [pl.ds(c * CHUNK, CHUNK) for c in chunks]
    z = _mm(al_ref[...], wa2_ref[...]) + ba_ref[...]
    la = -_softplus(-z) * (1.0 / GLA_GATE_NORM)
    g = [_cumsum_rows(la[c * CHUNK:(c + 1) * CHUNK], tril_b) for c in chunks]
    g_last = [g[c][CHUNK - 1:CHUNK, :] for c in chunks]
    q_dec = [q_ref[sl[c], :] * jnp.exp(g[c]) * (GLA_DK ** -0.5) for c in chunks]
    k_inv = [k_ref[sl[c], :] * jnp.exp(-g[c]) for c in chunks]
    k_end = [k_ref[sl[c], :] * jnp.exp(g_last[c] - g[c]) for c in chunks]
    scores = [jnp.where(tril, _mm_nt(q_dec[c], k_inv[c]), 0.0) for c in chunks]
    kv = [_mm_tn(k_end[c], v_ref[sl[c], :]) for c in chunks]
    o_intra = [_mm(scores[c], v_ref[sl[c], :]) for c in chunks]
    dec_col = [jnp.sum(jnp.where(eye, jnp.exp(g_last[c]), 0.0), axis=1, keepdims=True) for c in chunks]
    s_cur = s_ref[...]
    for c in chunks:
        o = o_intra[c] + _mm(q_dec[c], s_cur)
        s_cur = s_cur * dec_col[c] + kv[c]
        o = o * lax.rsqrt(jnp.mean(o * o, -1, keepdims=True) + GLA_NORM_EPS) * ng_ref[...]
        o_ref[sl[c], :] = (o * _silu(r_ref[sl[c], :])).astype(o_ref.dtype)
    s_ref[...] = s_cur

    @pl.when(i == pl.num_programs(2) - 1)
    def _():
        sn_ref[0, 0] = s_ref[...]


def _gla_call(u, s0, wa2p, ba, ng, b, l):
    lb = min(l, 512)
    nb = l // lb
    row = lambda bb, h, i: bb * nb + i
    spec = lambda w, off: pl.BlockSpec((lb, w), lambda bb, h, i: (row(bb, h, i), off + h))
    return pl.pallas_call(
        functools.partial(_gla_kernel, nc=lb // CHUNK),
        name="gla_mixer",
        grid=(b, GLA_HEADS, nb),
        in_specs=[spec(128, OFF_GLA // 128), spec(128, (OFF_GLA + 512) // 128),
                  spec(256, (OFF_GLA + 1024) // 256), spec(256, (OFF_GLA + 2048) // 256),
                  pl.BlockSpec((lb, 128), lambda bb, h, i: (row(bb, h, i), OFF_GALOW // 128)),
                  pl.BlockSpec((128, 128), lambda bb, h, i: (0, h)),
                  pl.BlockSpec((1, 128), lambda bb, h, i: (0, h)),
                  pl.BlockSpec((1, 256), lambda bb, h, i: (0, 0)),
                  pl.BlockSpec((1, 1, GLA_DK, GLA_DV), lambda bb, h, i: (bb, h, 0, 0))],
        out_specs=[pl.BlockSpec((lb, 256), lambda bb, h, i: (row(bb, h, i), h)),
                   pl.BlockSpec((1, 1, GLA_DK, GLA_DV), lambda bb, h, i: (bb, h, 0, 0))],
        out_shape=[jax.ShapeDtypeStruct((b * l, GLA_HEADS * GLA_DV), BF16),
                   jax.ShapeDtypeStruct((b, GLA_HEADS, GLA_DK, GLA_DV), F32)],
        scratch_shapes=[pltpu.VMEM((GLA_DK, GLA_DV), F32)],
        compiler_params=pltpu.CompilerParams(dimension_semantics=("parallel", "parallel", "arbitrary")),
    )(u, u, u, u, u, wa2p, ba, ng, s0)


def _half_sum(x, lo):
    s0 = jnp.sum(jnp.where(lo, x, 0.0), -1, keepdims=True)
    s1 = jnp.sum(jnp.where(lo, 0.0, x), -1, keepdims=True)
    return jnp.where(lo, s0, s1)


def _rwkv_kernel(r_ref, k_ref, v_ref, low_ref, shr_ref, shk_ref, shv_ref, shl_ref,
                 mur_ref, muk_ref, muv_ref, mul_ref, w0_ref, w2_ref, a0_ref, a2_ref, g2_ref,
                 kk_ref, ka_ref, rk_ref, lng_ref, lnb_ref, s0_ref,
                 o_ref, sn_ref, s_ref, sh_ref, *, nc):
    i = pl.program_id(2)
    lb = nc * CHUNK
    hd = RWKV_HEAD

    @pl.when(i == 0)
    def _():
        s_ref[...] = s0_ref[0]
        sh_ref[7:8, 0:128] = shr_ref[0]
        sh_ref[7:8, 128:256] = shk_ref[0]
        sh_ref[7:8, 256:384] = shv_ref[0]
        sh_ref[7:8, 384:640] = shl_ref[0]

    sh_ref[8:8 + lb, 0:128] = r_ref[...]
    sh_ref[8:8 + lb, 128:256] = k_ref[...]
    sh_ref[8:8 + lb, 256:384] = v_ref[...]
    sh_ref[8:8 + lb, 384:640] = low_ref[...]

    tril = _tril_mask(CHUNK)
    stril = _tril_mask(CHUNK, strict=True)
    tril_b = tril.astype(BF16)
    eye_f = (lax.broadcasted_iota(jnp.int32, (CHUNK, CHUNK), 0)
             == lax.broadcasted_iota(jnp.int32, (CHUNK, CHUNK), 1)).astype(F32)
    lo = lax.broadcasted_iota(jnp.int32, (1, 128), 1) < hd

    prep = []
    for c in range(nc):
        cur = sh_ref[8 + c * CHUNK:8 + (c + 1) * CHUNK, :]
        prev = sh_ref[7 + c * CHUNK:7 + (c + 1) * CHUNK, :]
        shift = lambda a, b, mu: a + (b - a) * mu
        r = shift(cur[:, 0:128], prev[:, 0:128], mur_ref[...])
        k = shift(cur[:, 128:256], prev[:, 128:256], muk_ref[...])
        v = shift(cur[:, 256:384], prev[:, 256:384], muv_ref[...])
        low = shift(cur[:, 384:640], prev[:, 384:640], mul_ref[...])
        zw = w0_ref[...] + _mm(jnp.tanh(low[:, 0:64]), w2_ref[...])
        lw = -jnp.exp(-_softplus(-zw) - 0.5)
        a = _sigmoid(a0_ref[...] + _mm(low[:, 64:128], a2_ref[...]))
        gate = _mm(_sigmoid(low[:, 128:256]), g2_ref[...])
        kk = k * kk_ref[...]
        kk = kk * lax.rsqrt(jnp.maximum(_half_sum(kk * kk, lo), 1e-24))
        k2 = k * (1.0 + (a - 1.0) * ka_ref[...])
        lg = _cumsum_rows(lw, tril_b)
        e_neg = jnp.exp(-lg)
        prep.append(dict(a=-kk * jnp.exp(lg - lw), b=kk * a * e_neg, k=k2 * e_neg, r=r * jnp.exp(lg),
                         v=v, gam=jnp.exp(lg[CHUNK - 1:CHUNK, :]), gate=gate,
                         bonus=_half_sum(r * k2 * rk_ref[...], lo) * v))

    units = [(c, hh) for c in range(nc) for hh in range(2)]
    head = lambda name: {(c, hh): prep[c][name][:, hh * hd:(hh + 1) * hd] for c, hh in units}
    at, bt, kt, rt, vv, gam = head("a"), head("b"), head("k"), head("r"), head("v"), head("gam")
    ar = {u: jnp.concatenate([at[u], rt[u]], axis=0) for u in units}
    pb = {u: _mm_nt(ar[u], bt[u]) for u in units}
    pk = {u: _mm_nt(ar[u], kt[u]) for u in units}
    n_ab = {u: jnp.where(stril, pb[u][:CHUNK], 0.0) for u in units}
    m_rb = {u: jnp.where(tril, pb[u][CHUNK:], 0.0) for u in units}
    n_ak = {u: jnp.where(stril, pk[u][:CHUNK], 0.0) for u in units}
    m_rk = {u: jnp.where(tril, pk[u][CHUNK:], 0.0) for u in units}
    t_inv = {u: eye_f + n_ab[u] for u in units}
    pw = n_ab
    for _ in range(5):
        pw = {u: _mm(pw[u], pw[u]) for u in units}
        t_inv = {u: t_inv[u] + _mm(t_inv[u], pw[u]) for u in units}
    akv = {u: _mm(n_ak[u], vv[u]) for u in units}
    a_hat = {u: _mm(t_inv[u], at[u]) for u in units}
    u0 = {u: _mm(t_inv[u], akv[u]) for u in units}
    g_lr = {u: _mm_tn(a_hat[u], bt[u]) * gam[u] for u in units}
    c0 = {u: _mm_tn(jnp.concatenate([u0[u], vv[u]], axis=0),
                    jnp.concatenate([bt[u], kt[u]], axis=0)) * gam[u] for u in units}
    r_hat = {u: rt[u] + _mm(m_rb[u], a_hat[u]) for u in units}
    y0 = {u: _mm(m_rb[u], u0[u]) + _mm(m_rk[u], vv[u]) for u in units}

    s_cur = [s_ref[0], s_ref[1]]
    for c in range(nc):
        ys = []
        for hh in range(2):
            u = (c, hh)
            s0 = s_cur[hh]
            ys.append(_mm_nt(r_hat[u], s0) + y0[u])
            s_cur[hh] = s0 * gam[u] + _mm(s0, g_lr[u]) + c0[u]
        y = jnp.concatenate(ys, axis=1)
        mean = _half_sum(y, lo) * (1.0 / hd)
        yc = y - mean
        var = _half_sum(yc * yc, lo) * (1.0 / hd)
        y = yc * lax.rsqrt(var + RWKV_GN_EPS) * lng_ref[...] + lnb_ref[...]
        o_ref[c * CHUNK:(c + 1) * CHUNK, :] = ((y + prep[c]["bonus"]) * prep[c]["gate"]).astype(o_ref.dtype)
    s_ref[0] = s_cur[0]
    s_ref[1] = s_cur[1]

    sh_ref[7:8, :] = sh_ref[7 + lb:8 + lb, :]

    @pl.when(i == pl.num_programs(2) - 1)
    def _():
        sn_ref[0] = s_ref[...]


def _rwkv_call(u, shift_buf, s0, p, b, l):
    lb = min(l, 512)
    nb = l // lb
    row = lambda bb, h, i: bb * nb + i
    ublk = lambda off: pl.BlockSpec((lb, 128), lambda bb, h, i: (row(bb, h, i), off // 128 + h))
    sblk = lambda off: pl.BlockSpec((1, 1, 128), lambda bb, h, i: (bb, 0, off // 128 + h))
    vec = lambda: pl.BlockSpec((1, 128), lambda bb, h, i: (0, h))
    const = lambda shape: pl.BlockSpec(shape, lambda bb, h, i: (0,) * len(shape))
    return pl.pallas_call(
        functools.partial(_rwkv_kernel, nc=lb // CHUNK),
        name="rwkv_mixer",
        grid=(b, RWKV_HEADS // 2, nb),
        in_specs=[ublk(OFF_RWKV), ublk(OFF_RWKV + 1024), ublk(OFF_RWKV + 2048),
                  pl.BlockSpec((lb, 256), lambda bb, h, i: (row(bb, h, i), OFF_RLOW // 256)),
                  sblk(0), sblk(1024), sblk(2048),
                  pl.BlockSpec((1, 1, 256), lambda bb, h, i: (bb, 0, 3072 // 256)),
                  vec(), pl.BlockSpec((1, 128), lambda bb, h, i: (0, 8 + h)),
                  pl.BlockSpec((1, 128), lambda bb, h, i: (0, 16 + h)),
                  pl.BlockSpec((1, 256), lambda bb, h, i: (0, 3072 // 256)),
                  vec(), pl.BlockSpec((64, 128), lambda bb, h, i: (0, h)),
                  vec(), pl.BlockSpec((64, 128), lambda bb, h, i: (0, h)),
                  pl.BlockSpec((128, 128), lambda bb, h, i: (0, h)),
                  vec(), vec(), vec(), vec(), vec(),
                  pl.BlockSpec((1, 2, RWKV_HEAD, RWKV_HEAD), lambda bb, h, i: (bb, h, 0, 0))],
        out_specs=[pl.BlockSpec((lb, 128), lambda bb, h, i: (row(bb, h, i), h)),
                   pl.BlockSpec((1, 2, RWKV_HEAD, RWKV_HEAD), lambda bb, h, i: (bb, h, 0, 0))],
        out_shape=[jax.ShapeDtypeStruct((b * l, RWKV_DIM), BF16),
                   jax.ShapeDtypeStruct((b, RWKV_HEADS, RWKV_HEAD, RWKV_HEAD), F32)],
        scratch_shapes=[pltpu.VMEM((2, RWKV_HEAD, RWKV_HEAD), F32),
                        pltpu.VMEM((lb + 8, 640), F32)],
        compiler_params=pltpu.CompilerParams(dimension_semantics=("parallel", "parallel", "arbitrary")),
    )(u, u, u, u, shift_buf, shift_buf, shift_buf, shift_buf,
      p["mu"], p["mu"], p["mu"], p["mu"], p["w0"], p["w2"], p["a0"], p["a2"], p["g2"],
      p["k_k"], p["k_a"], p["r_k"], p["ln_g"], p["ln_b"], s0)


def _ssd_kernel(z_ref, x_ref, bm_ref, cm_ref, dt_ref, cbx_ref, cbb_ref, cbc_ref,
                cwx_ref, cwb_ref, cwc_ref, cbiasx_ref, cbiasb_ref, cbiasc_ref,
                dtb_ref, alog_ref, dsk_ref, ng_ref, s0_ref,
                o_ref, sn_ref, s_ref, full_ref, *, nc):
    i = pl.program_id(2)
    lb = nc * CHUNK
    hpg = SSM_HEADS // SSM_GROUPS
    p = SSM_HEADDIM
    wx = hpg * p

    @pl.when(i == 0)
    def _():
        s_ref[...] = s0_ref[0].reshape(hpg * p, SSM_STATE)
        full_ref[5:8, 0:wx] = cbx_ref[0]
        full_ref[5:8, wx:wx + 128] = cbb_ref[0]
        full_ref[5:8, wx + 128:wx + 256] = cbc_ref[0]

    rnd = lambda a: a.astype(BF16).astype(F32)

    @pl.when(i == 0)
    def _():
        full_ref[5:8, :] = rnd(full_ref[5:8, :])

    full_ref[8:8 + lb, 0:wx] = rnd(x_ref[...])
    full_ref[8:8 + lb, wx:wx + 128] = rnd(bm_ref[...])
    full_ref[8:8 + lb, wx + 128:wx + 256] = rnd(cm_ref[...])

    tril = _tril_mask(CHUNK)
    tril_b = tril.astype(BF16)
    cw = jnp.concatenate([cwx_ref[...], cwb_ref[...], cwc_ref[...]], axis=1)
    cbias = jnp.concatenate([cbiasx_ref[...], cbiasb_ref[...], cbiasc_ref[...]], axis=1)
    a_row = -jnp.exp(alog_ref[0])

    iota = lambda shape, d: lax.broadcasted_iota(jnp.int32, shape, d)
    head_of = lambda idx: lax.shift_right_logical(idx, jnp.int32(6))
    expand = (iota((128, wx), 0) == head_of(iota((128, wx), 1))).astype(BF16)
    select = (head_of(iota((wx, 128), 0)) == iota((wx, 128), 1)).astype(BF16)
    pos = jnp.bitwise_and(iota((CHUNK, wx), 1), jnp.int32(CHUNK - 1))
    tril_t = iota((CHUNK, wx), 0) >= pos
    diag_t = iota((CHUNK, wx), 0) == pos
    blockdiag = head_of(iota((wx, wx), 0)) == head_of(iota((wx, wx), 1))
    ones_b = jnp.ones((CHUNK, CHUNK), BF16)

    def split3(x):
        x1 = x.astype(BF16)
        r1 = x - x1.astype(F32)
        x2 = r1.astype(BF16)
        return x1, x2, (r1 - x2.astype(F32)).astype(BF16)

    dot = lambda a, b: jnp.dot(a, b, preferred_element_type=F32)
    dot_nt = lambda a, b: lax.dot_general(a, b, (((1,), (1,)), ((), ())), preferred_element_type=F32)
    per_head = lambda x: sum(dot(piece, expand) for piece in split3(x))

    chunks = range(nc)
    xs, bm, cm, dtv, acum = [], [], [], [], []
    for c in chunks:
        base = c * CHUNK
        conv = cbias
        for j in range(SSM_CONV):
            conv = conv + cw[j:j + 1, :] * full_ref[5 + j + base:5 + j + base + CHUNK, :]
        conv = _silu(conv)
        xs.append(conv[:, 0:wx])
        bm.append(conv[:, wx:wx + 128])
        cm.append(conv[:, wx + 128:wx + 256])
        dtv.append(_softplus(dt_ref[base:base + CHUNK, :] + dtb_ref[0]))
    acum = [_cumsum_rows(dtv[c] * a_row, tril_b) for c in chunks]
    a_col = [per_head(acum[c]) for c in chunks]
    a_row_e = [sum(dot(ones_b, piece) for piece in split3(jnp.where(diag_t, a_col[c], 0.0)))
               for c in chunks]
    dt_e = [per_head(dtv[c]) for c in chunks]
    cb_t = [_mm_nt(cm[c], jnp.concatenate([bm[c]] * hpg, axis=0)) for c in chunks]
    dec = [jnp.where(tril_t, jnp.exp(jnp.where(tril_t, a_col[c] - a_row_e[c], 0.0)), 0.0) for c in chunks]
    xdt = [xs[c] * dt_e[c] for c in chunks]
    xdt_bd = [jnp.where(blockdiag, jnp.concatenate([xdt[c].astype(BF16)] * hpg, axis=0),
                        jnp.zeros((), BF16)) for c in chunks]
    y_diag = [dot((cb_t[c] * dec[c]).astype(BF16), xdt_bd[c]) for c in chunks]
    a_last = [a_col[c][CHUNK - 1:CHUNK, :] for c in chunks]
    st = [_mm_tn(xdt[c] * jnp.exp(a_last[c] - a_col[c]), bm[c]) for c in chunks]
    dcol = [jnp.exp(sum(dot_nt(select, piece) for piece in
                        split3(jnp.broadcast_to(acum[c][CHUNK - 1:CHUNK, :], (128, 128)))))
            for c in chunks]
    e_in = [jnp.exp(a_col[c]) for c in chunks]
    s_cur = s_ref[...]
    for c in chunks:
        base = c * CHUNK
        y = y_diag[c] + _mm_nt(cm[c], s_cur) * e_in[c] + xs[c] * dsk_ref[...]
        s_cur = s_cur * dcol[c] + st[c]
        y = y * _silu(z_ref[base:base + CHUNK, :])
        y = y * lax.rsqrt(jnp.mean(y * y, -1, keepdims=True) + SSM_NORM_EPS) * ng_ref[...]
        o_ref[base:base + CHUNK, :] = y.astype(o_ref.dtype)
    s_ref[...] = s_cur

    full_ref[5:8, :] = full_ref[5 + lb:8 + lb, :]

    @pl.when(i == pl.num_programs(2) - 1)
    def _():
        sn_ref[0] = s_ref[...].reshape(hpg, p, SSM_STATE)


def _ssd_call(u, conv_buf, s0, p, b, l):
    lb = min(l, 256)
    nb = l // lb
    hpg = SSM_HEADS // SSM_GROUPS
    row = lambda bb, g, i: bb * nb + i
    ublk = lambda w, off: pl.BlockSpec((lb, w), lambda bb, g, i: (row(bb, g, i), off // w + g))
    gvec = lambda w, off: pl.BlockSpec((1, w), lambda bb, g, i: (0, off // w + g))
    gtap = lambda w, off: pl.BlockSpec((SSM_CONV, w), lambda bb, g, i: (0, off // w + g))
    gbuf = lambda w, off: pl.BlockSpec((1, SSM_CONV - 1, w), lambda bb, g, i: (bb, 0, off // w + g))
    return pl.pallas_call(
        functools.partial(_ssd_kernel, nc=lb // CHUNK),
        name="ssd_mixer",
        grid=(b, SSM_GROUPS, nb),
        in_specs=[ublk(512, OFF_SSM), ublk(512, OFF_SSM + 2048),
                  ublk(128, OFF_SSM + 4096), ublk(128, OFF_SSM + 4608), ublk(128, OFF_DT),
                  gbuf(512, 0), gbuf(128, 2048), gbuf(128, 2560),
                  gtap(512, 0), gtap(128, 2048), gtap(128, 2560),
                  gvec(512, 0), gvec(128, 2048), gvec(128, 2560),
                  pl.BlockSpec((1, 1, 128), lambda bb, g, i: (g, 0, 0)),
                  pl.BlockSpec((1, 1, 128), lambda bb, g, i: (g, 0, 0)),
                  gvec(512, 0), gvec(512, 0),
                  pl.BlockSpec((1, hpg, SSM_HEADDIM, SSM_STATE), lambda bb, g, i: (bb, g, 0, 0))],
        out_specs=[pl.BlockSpec((lb, 512), lambda bb, g, i: (row(bb, g, i), g)),
                   pl.BlockSpec((1, hpg, SSM_HEADDIM, SSM_STATE), lambda bb, g, i: (bb, g, 0, 0))],
        out_shape=[jax.ShapeDtypeStruct((b * l, SSM_INNER), BF16),
                   jax.ShapeDtypeStruct((b, SSM_HEADS, SSM_HEADDIM, SSM_STATE), F32)],
        scratch_shapes=[pltpu.VMEM((hpg * SSM_HEADDIM, SSM_STATE), F32),
                        pltpu.VMEM((lb + 8, 768), F32)],
        compiler_params=pltpu.CompilerParams(dimension_semantics=("parallel", "parallel", "arbitrary")),
    )(u, u, u, u, u, conv_buf, conv_buf, conv_buf,
      p["conv_w"], p["conv_w"], p["conv_w"], p["conv_b"], p["conv_b"], p["conv_b"],
      p["dt_bias"], p["a_log"], p["d_skip"], p["norm_g"], s0)


def _merge_kernel(gate_ref, og_ref, or_ref, os_ref, x_ref, wg_ref, wr_ref, ws_ref, wo_ref,
                  g_ref, b_ref, o_ref):
    d = D_MODEL
    m = _sigmoid(gate_ref[:, 0:d]) * jnp.dot(og_ref[...], wg_ref[...], preferred_element_type=F32)
    m = m + _sigmoid(gate_ref[:, d:2 * d]) * jnp.dot(or_ref[...], wr_ref[...], preferred_element_type=F32)
    m = m + _sigmoid(gate_ref[:, 2 * d:3 * d]) * jnp.dot(os_ref[...], ws_ref[...], preferred_element_type=F32)
    y = DN_ALPHA * x_ref[...] + _mm(m, wo_ref[...])
    o_ref[...] = _layer_norm(y, g_ref[...], b_ref[...])


def _merge_call(u, og, orr, os_, x, wg, wr, ws, wo, g, b):
    t = x.shape[0]
    tm = min(t, 256)
    tok = lambda w: pl.BlockSpec((tm, w), lambda i: (i, 0))
    const = lambda r, c: pl.BlockSpec((r, c), lambda i: (0, 0))
    return pl.pallas_call(
        _merge_kernel,
        name="merge_outproj_ln",
        grid=(t // tm,),
        in_specs=[tok(3 * D_MODEL), tok(D_MODEL), tok(D_MODEL), tok(SSM_INNER), tok(D_MODEL),
                  const(D_MODEL, D_MODEL), const(D_MODEL, D_MODEL), const(SSM_INNER, D_MODEL),
                  const(D_MODEL, D_MODEL), const(1, D_MODEL), const(1, D_MODEL)],
        out_specs=tok(D_MODEL),
        out_shape=jax.ShapeDtypeStruct((t, D_MODEL), F32),
        compiler_params=pltpu.CompilerParams(dimension_semantics=("parallel",),
                                             vmem_limit_bytes=VMEM_LIMIT),
    )(u, og, orr, os_, x, wg, wr, ws, wo, g, b)


def _router_gates(xb, wrt_ref, rb_ref, key_ref):
    tm = xb.shape[0]
    per_group = N_EXPERTS // N_EXPERT_GROUPS
    s_t = _sigmoid(_mm_nt(wrt_ref[...], xb))
    sb = s_t + rb_ref[...]
    gscore = []
    for g in range(N_EXPERT_GROUPS):
        xg = sb[g * per_group:(g + 1) * per_group, :]
        m1 = jnp.max(xg, axis=0, keepdims=True)
        eq = xg == m1
        cnt = jnp.sum(eq.astype(F32), axis=0, keepdims=True)
        m2 = jnp.max(jnp.where(eq, -jnp.inf, xg), axis=0, keepdims=True)
        gscore.append(m1 + jnp.where(cnt >= 2.0, m1, m2))
    for g in range(N_EXPERT_GROUPS):
        rank = jnp.zeros((1, tm), F32)
        for g2 in range(N_EXPERT_GROUPS):
            if g2 == g:
                continue
            beats = (gscore[g2] >= gscore[g]) if g2 < g else (gscore[g2] > gscore[g])
            rank = rank + beats.astype(F32)
        keep = rank < float(TOPK_GROUPS)
        key_ref[g * per_group:(g + 1) * per_group, :] = jnp.where(
            keep, sb[g * per_group:(g + 1) * per_group, :], -jnp.inf)
    key = key_ref[...]
    eidx = lax.broadcasted_iota(jnp.int32, (N_EXPERTS, tm), 0)

    def body(ep, rank):
        rowv = key_ref[pl.ds(ep, 1), :]
        beats = (rowv > key) | ((rowv == key) & (ep < eidx))
        return rank + beats.astype(F32)

    rank = lax.fori_loop(0, N_EXPERTS, body, jnp.zeros((N_EXPERTS, tm), F32))
    w = jnp.where(rank < float(TOP_K), s_t, 0.0)
    return w / jnp.sum(w, axis=0, keepdims=True) * ROUTED_SCALE


def _moe_kernel(x_ref, wrt_ref, rb_ref, w1_ref, w3_ref, w2_ref, s1_ref, s3_ref, s2_ref,
                g_ref, b_ref, o_ref, xb_ref, gates_ref, key_ref, *, sub):
    e = pl.program_id(1)
    tm = x_ref.shape[0]

    @pl.when(e == 0)
    def _():
        xb_ref[...] = x_ref[...].astype(BF16)
        g_t = _router_gates(xb_ref[...], wrt_ref, rb_ref, key_ref)
        gates_ref[...] = jnp.transpose(jnp.concatenate([g_t, jnp.zeros_like(g_t)], axis=0))
        for s in range(tm // sub):
            rows = pl.ds(s * sub, sub)
            xs = xb_ref[rows, :]
            hid = _silu(jnp.dot(xs, s1_ref[...], preferred_element_type=F32)) * jnp.dot(
                xs, s3_ref[...], preferred_element_type=F32)
            o_ref[rows, :] = _mm(hid, s2_ref[...])

    lane = lax.broadcasted_iota(jnp.int32, (1, 128), 1)
    for s in range(tm // sub):
        rows = pl.ds(s * sub, sub)
        xs = xb_ref[rows, :]
        gcol = jnp.sum(jnp.where(lane == e, gates_ref[rows, :], 0.0), axis=1, keepdims=True)
        hid = _silu(jnp.dot(xs, w1_ref[0], preferred_element_type=F32)) * jnp.dot(
            xs, w3_ref[0], preferred_element_type=F32)
        o_ref[rows, :] += _mm(hid * gcol, w2_ref[0])

    @pl.when(e == pl.num_programs(1) - 1)
    def _():
        o_ref[...] = _layer_norm(DN_ALPHA * x_ref[...] + o_ref[...], g_ref[...], b_ref[...])


def _moe_call(h, wrt, rb, w1, w3, w2, s1, s3, s2, g, b):
    t = h.shape[0]
    tm = min(t, 2048)
    sub = min(tm, 512)
    const = lambda r, c: pl.BlockSpec((r, c), lambda i, e: (0, 0))
    return pl.pallas_call(
        functools.partial(_moe_kernel, sub=sub),
        name="moe_ln",
        grid=(t // tm, N_EXPERTS),
        in_specs=[pl.BlockSpec((tm, D_MODEL), lambda i, e: (i, 0)),
                  const(N_EXPERTS, D_MODEL), const(N_EXPERTS, 1),
                  pl.BlockSpec((1, D_MODEL, D_EXPERT), lambda i, e: (e, 0, 0)),
                  pl.BlockSpec((1, D_MODEL, D_EXPERT), lambda i, e: (e, 0, 0)),
                  pl.BlockSpec((1, D_EXPERT, D_MODEL), lambda i, e: (e, 0, 0)),
                  const(D_MODEL, D_EXPERT), const(D_MODEL, D_EXPERT), const(D_EXPERT, D_MODEL),
                  const(1, D_MODEL), const(1, D_MODEL)],
        out_specs=pl.BlockSpec((tm, D_MODEL), lambda i, e: (i, 0)),
        out_shape=jax.ShapeDtypeStruct((t, D_MODEL), F32),
        scratch_shapes=[pltpu.VMEM((tm, D_MODEL), BF16),
                        pltpu.VMEM((tm, 128), F32),
                        pltpu.VMEM((N_EXPERTS, tm), F32)],
        compiler_params=pltpu.CompilerParams(dimension_semantics=("parallel", "arbitrary"),
                                             vmem_limit_bytes=VMEM_LIMIT),
    )(h, wrt, rb, w1, w3, w2, s1, s3, s2, g, b)


def _prep_layer(w_in, gla_wa2, gla_ba, gla_norm_g, gla_wo,
                rwkv_mu, rwkv_w0, rwkv_w2, rwkv_a0, rwkv_a2, rwkv_g2, rwkv_k_k, rwkv_k_a, rwkv_r_k,
                rwkv_ln_g, rwkv_ln_b, rwkv_wo,
                ssm_conv_w, ssm_conv_b, ssm_dt_bias, ssm_a_log, ssm_d, ssm_norm_g, ssm_wo,
                w_out, ln1_g, ln1_b,
                w_router, router_bias, exp_w1, exp_w3, exp_w2, sh_w1, sh_w3, sh_w2, ln2_g, ln2_b):
    d = D_MODEL
    o_gla, o_rwkv = 3 * d, 3 * d + 3088
    o_ssm = o_rwkv + RWKV_PROJ
    zeros = lambda n: jnp.zeros((d, n), w_in.dtype)
    dt0 = o_ssm + SSM_INNER + SSM_CONV_DIM
    hpg = SSM_HEADS // SSM_GROUPS
    dt_cols = []
    for g in range(SSM_GROUPS):
        dt_cols += [w_in[:, dt0 + g * hpg:dt0 + (g + 1) * hpg], zeros(128 - hpg)]
    w_u = jnp.concatenate(
        [w_in[:, 0:3 * d], w_in[:, o_gla:o_gla + 3072], w_in[:, o_rwkv:o_rwkv + 3072],
         w_in[:, o_ssm:o_ssm + SSM_INNER + SSM_CONV_DIM], w_in[:, o_rwkv + 3072:o_rwkv + 3328],
         w_in[:, o_gla + 3072:o_gla + 3088], zeros(128 - GLA_RANK)] + dt_cols + [zeros(128)],
        axis=1).astype(BF16)
    assert w_u.shape[1] == N_U
    row = lambda v: v.reshape(1, -1)
    pad_groups = lambda v: jnp.pad(v.reshape(SSM_GROUPS, 1, hpg), ((0, 0), (0, 0), (0, 128 - hpg)))
    return dict(
        w_u=w_u,
        gla=dict(wa2=jnp.pad(gla_wa2, ((0, 128 - GLA_RANK), (0, 0))).astype(BF16),
                 ba=row(gla_ba), ng=row(gla_norm_g)),
        rwkv=dict(mu=row(rwkv_mu), w0=row(rwkv_w0), w2=rwkv_w2.astype(BF16), a0=row(rwkv_a0),
                  a2=rwkv_a2.astype(BF16), g2=rwkv_g2.astype(BF16), k_k=row(rwkv_k_k),
                  k_a=row(rwkv_k_a), r_k=row(rwkv_r_k), ln_g=row(rwkv_ln_g), ln_b=row(rwkv_ln_b)),
        ssm=dict(conv_w=ssm_conv_w, conv_b=row(ssm_conv_b), dt_bias=pad_groups(ssm_dt_bias),
                 a_log=pad_groups(ssm_a_log), d_skip=row(jnp.repeat(ssm_d, SSM_HEADDIM)),
                 norm_g=row(ssm_norm_g)),
        merge=(gla_wo.astype(BF16), rwkv_wo.astype(BF16), ssm_wo.astype(BF16), w_out.astype(BF16),
               row(ln1_g), row(ln1_b)),
        moe=(jnp.transpose(w_router).astype(BF16), router_bias.reshape(-1, 1),
             exp_w1.astype(BF16), exp_w3.astype(BF16), exp_w2.astype(BF16),
             sh_w1.astype(BF16), sh_w3.astype(BF16), sh_w2.astype(BF16), row(ln2_g), row(ln2_b)),
    )


def _layer(h, st_gla, st_rwkv, st_shift, st_ssm, st_conv, lp, b, l):
    u = _inproj_call(h, lp["w_u"])
    o_gla, s_gla = _gla_call(u, st_gla, lp["gla"]["wa2"], lp["gla"]["ba"], lp["gla"]["ng"], b, l)
    o_rwkv, s_rwkv = _rwkv_call(u, st_shift, st_rwkv, lp["rwkv"], b, l)
    o_ssm, s_ssm = _ssd_call(u, st_conv, st_ssm, lp["ssm"], b, l)
    h1 = _merge_call(u, o_gla, o_rwkv, o_ssm, h, *lp["merge"])
    h2 = _moe_call(h1, *lp["moe"])
    u3 = u.reshape(b, l, N_U)
    shift_new = jnp.concatenate([u3[:, l - 1:, OFF_RWKV:OFF_RWKV + 3072],
                                 u3[:, l - 1:, OFF_RLOW:OFF_RLOW + 256]], axis=-1)
    conv_new = u3[:, l - (SSM_CONV - 1):, OFF_SSM + SSM_INNER:OFF_SSM + SSM_INNER + SSM_CONV_DIM]
    return h2, s_gla, s_rwkv, shift_new, s_ssm, conv_new


def _zero_states(batch):
    return (jnp.zeros((batch, GLA_HEADS, GLA_DK, GLA_DV), F32),
            jnp.zeros((batch, RWKV_HEADS, RWKV_HEAD, RWKV_HEAD), F32),
            jnp.zeros((batch, 1, RWKV_PROJ), F32),
            jnp.zeros((batch, SSM_HEADS, SSM_HEADDIM, SSM_STATE), F32),
            jnp.zeros((batch, SSM_CONV - 1, SSM_CONV_DIM), F32))


def kernel(x_prompt, x_sample, state_gla, state_rwkv, state_rwkv_shift, state_ssm, state_ssm_conv,
           ln_in_g, ln_in_b, w_in, gla_wa2, gla_ba, gla_norm_g, gla_wo,
           rwkv_mu, rwkv_w0, rwkv_w2, rwkv_a0, rwkv_a2, rwkv_g2, rwkv_k_k, rwkv_k_a, rwkv_r_k,
           rwkv_ln_g, rwkv_ln_b, rwkv_wo,
           ssm_conv_w, ssm_conv_b, ssm_dt_bias, ssm_a_log, ssm_d, ssm_norm_g, ssm_wo,
           w_out, ln1_g, ln1_b,
           w_router, router_bias, exp_w1, exp_w3, exp_w2, sh_w1, sh_w3, sh_w2, ln2_g, ln2_b):
    layer_params = (w_in, gla_wa2, gla_ba, gla_norm_g, gla_wo,
                    rwkv_mu, rwkv_w0, rwkv_w2, rwkv_a0, rwkv_a2, rwkv_g2, rwkv_k_k, rwkv_k_a, rwkv_r_k,
                    rwkv_ln_g, rwkv_ln_b, rwkv_wo,
                    ssm_conv_w, ssm_conv_b, ssm_dt_bias, ssm_a_log, ssm_d, ssm_norm_g, ssm_wo,
                    w_out, ln1_g, ln1_b,
                    w_router, router_bias, exp_w1, exp_w3, exp_w2, sh_w1, sh_w3, sh_w2, ln2_g, ln2_b)
    depth = w_in.shape[0]
    bp, lp_, d = x_prompt.shape
    bs, ls, _ = x_sample.shape
    h_p = _ln_call(x_prompt.reshape(bp * lp_, d), ln_in_g, ln_in_b)
    h_s = _ln_call(x_sample.reshape(bs * ls, d), ln_in_g, ln_in_b)
    new_p, new_s = [], []
    for i in range(depth):
        lp = _prep_layer(*[t[i] for t in layer_params])
        h_p, *st_p = _layer(h_p, *_zero_states(bp), lp, bp, lp_)
        new_p.append(st_p)
        h_s, *st_s = _layer(h_s, state_gla[i], state_rwkv[i], state_rwkv_shift[i], state_ssm[i],
                            state_ssm_conv[i], lp, bs, ls)
        new_s.append(st_s)
    stack = lambda rows: tuple(jnp.stack(z) for z in zip(*rows))
    return (h_p.reshape(bp, lp_, d), h_s.reshape(bs, ls, d)) + stack(new_p) + stack(new_s)
```

```python
import functools

import jax
import jax.numpy as jnp
from jax import lax
from jax.experimental import pallas as pl
from jax.experimental.pallas import tpu as pltpu

F32 = jnp.float32
BF16 = jnp.bfloat16

D_MODEL = 1024
CHUNK = 64
GLA_HEADS, GLA_DK, GLA_DV, GLA_RANK = 4, 128, 256, 16
GLA_GATE_NORM = 16.0
GLA_NORM_EPS = 1e-5
RWKV_HEAD, RWKV_HEADS, RWKV_DIM = 64, 16, 1024
RWKV_PROJ = 3 * RWKV_DIM + 64 + 64 + 128
RWKV_GN_EPS = 64e-5
SSM_INNER, SSM_HEADDIM, SSM_HEADS, SSM_GROUPS, SSM_STATE, SSM_CONV = 2048, 64, 32, 4, 128, 4
SSM_CONV_DIM = SSM_INNER + 2 * SSM_GROUPS * SSM_STATE
SSM_NORM_EPS = 1e-5
N_EXPERTS, TOP_K, N_EXPERT_GROUPS, TOPK_GROUPS, D_EXPERT = 64, 8, 8, 4, 256
ROUTED_SCALE = 2.5
DEPTH = 2
DN_ALPHA = (2.0 * DEPTH) ** 0.25
LN_EPS = 1e-5

OFF_GATE, OFF_GLA, OFF_RWKV, OFF_SSM = 0, 3072, 6144, 9216
OFF_RLOW, OFF_GALOW, OFF_DT = 14336, 14592, 14720
N_U = 15360
VMEM_LIMIT = 56 * 1024 * 1024


def _mm(a, b):
    return jnp.dot(a.astype(BF16), b.astype(BF16), preferred_element_type=F32)


def _mm_nt(a, b):
    return lax.dot_general(a.astype(BF16), b.astype(BF16), (((1,), (1,)), ((), ())),
                           preferred_element_type=F32)


def _mm_tn(a, b):
    return lax.dot_general(a.astype(BF16), b.astype(BF16), (((0,), (0,)), ((), ())),
                           preferred_element_type=F32)


def _sigmoid(x):
    return 1.0 / (1.0 + jnp.exp(-x))


def _silu(x):
    return x * _sigmoid(x)


def _softplus(x):
    return jnp.maximum(x, 0.0) + jnp.log(1.0 + jnp.exp(-jnp.abs(x)))


def _layer_norm(x, g, b):
    mu = jnp.mean(x, -1, keepdims=True)
    xc = x - mu
    var = jnp.mean(xc * xc, -1, keepdims=True)
    return xc * lax.rsqrt(var + LN_EPS) * g + b


def _tril_mask(n, strict=False):
    r = lax.broadcasted_iota(jnp.int32, (n, n), 0)
    c = lax.broadcasted_iota(jnp.int32, (n, n), 1)
    return (r > c) if strict else (r >= c)


def _cumsum_rows(x, tril_bf16):
    x1 = x.astype(BF16)
    r1 = x - x1.astype(F32)
    x2 = r1.astype(BF16)
    x3 = (r1 - x2.astype(F32)).astype(BF16)
    dot = lambda p: jnp.dot(tril_bf16, p, preferred_element_type=F32)
    return dot(x1) + dot(x2) + dot(x3)


def _ln_kernel(x_ref, g_ref, b_ref, o_ref):
    o_ref[...] = _layer_norm(x_ref[...], g_ref[...], b_ref[...])


def _ln_call(x, g, b):
    t = x.shape[0]
    tm = min(t, 512)
    return pl.pallas_call(
        _ln_kernel,
        name="ln_in",
        grid=(t // tm,),
        in_specs=[pl.BlockSpec((tm, D_MODEL), lambda i: (i, 0)),
                  pl.BlockSpec((1, D_MODEL), lambda i: (0, 0)),
                  pl.BlockSpec((1, D_MODEL), lambda i: (0, 0))],
        out_specs=pl.BlockSpec((tm, D_MODEL), lambda i: (i, 0)),
        out_shape=jax.ShapeDtypeStruct((t, D_MODEL), F32),
        compiler_params=pltpu.CompilerParams(dimension_semantics=("parallel",)),
    )(x, g.reshape(1, -1), b.reshape(1, -1))


def _inproj_kernel(x_ref, w_ref, o_ref, xb_ref):
    @pl.when(pl.program_id(1) == 0)
    def _():
        xb_ref[...] = x_ref[...].astype(BF16)

    o_ref[...] = jnp.dot(xb_ref[...], w_ref[...], preferred_element_type=F32)


def _inproj_call(h, w):
    t, n = h.shape[0], w.shape[1]
    tm = min(t, 1024)
    tn = 1024
    return pl.pallas_call(
        _inproj_kernel,
        name="inproj",
        grid=(t // tm, n // tn),
        in_specs=[pl.BlockSpec((tm, D_MODEL), lambda i, j: (i, 0)),
                  pl.BlockSpec((D_MODEL, tn), lambda i, j: (0, j))],
        out_specs=pl.BlockSpec((tm, tn), lambda i, j: (i, j)),
        out_shape=jax.ShapeDtypeStruct((t, n), F32),
        scratch_shapes=[pltpu.VMEM((tm, D_MODEL), BF16)],
        compiler_params=pltpu.CompilerParams(dimension_semantics=("parallel", "arbitrary")),
    )(h, w)


def _gla_kernel(q_ref, k_ref, v_ref, r_ref, al_ref, wa2_ref, ba_ref, ng_ref, s0_ref,
                o_ref, sn_ref, s_ref, *, nc):
    i = pl.program_id(2)

    @pl.when(i == 0)
    def _():
        s_ref[...] = s0_ref[0, 0]

    tril = _tril_mask(CHUNK)
    tril_b = tril.astype(BF16)
    eye = (lax.broadcasted_iota(jnp.int32, (GLA_DK, GLA_DK), 0)
           == lax.broadcasted_iota(jnp.int32, (GLA_DK, GLA_DK), 1))
    chunks = range(nc)
    sl = [pl.ds(c * CHUNK, CHUNK) for c in chunks]
    z = _mm(al_ref[...], wa2_ref[...]) + ba_ref[...]
    la = -_softplus(-z) * (1.0 / GLA_GATE_NORM)
    g = [_cumsum_rows(la[c * CHUNK:(c + 1) * CHUNK], tril_b) for c in chunks]
    g_last = [g[c][CHUNK - 1:CHUNK, :] for c in chunks]
    q_dec = [q_ref[sl[c], :] * jnp.exp(g[c]) * (GLA_DK ** -0.5) for c in chunks]
    k_inv = [k_ref[sl[c], :] * jnp.exp(-g[c]) for c in chunks]
    k_end = [k_ref[sl[c], :] * jnp.exp(g_last[c] - g[c]) for c in chunks]
    scores = [jnp.where(tril, _mm_nt(q_dec[c], k_inv[c]), 0.0) for c in chunks]
    kv = [_mm_tn(k_end[c], v_ref[sl[c], :]) for c in chunks]
    o_intra = [_mm(scores[c], v_ref[sl[c], :]) for c in chunks]
    dec_col = [jnp.sum(jnp.where(eye, jnp.exp(g_last[c]), 0.0), axis=1, keepdims=True) for c in chunks]
    s_cur = s_ref[...]
    for c in chunks:
        o = o_intra[c] + _mm(q_dec[c], s_cur)
        s_cur = s_cur * dec_col[c] + kv[c]
        o = o * lax.rsqrt(jnp.mean(o * o, -1, keepdims=True) + GLA_NORM_EPS) * ng_ref[...]
        o_ref[sl[c], :] = (o * _silu(r_ref[sl[c], :])).astype(o_ref.dtype)
    s_ref[...] = s_cur

    @pl.when(i == pl.num_programs(2) - 1)
    def _():
        sn_ref[0, 0] = s_ref[...]


def _gla_call(u, s0, wa2p, ba, ng, b, l):
    lb = min(l, 512)
    nb = l // lb
    row = lambda bb, h, i: bb * nb + i
    spec = lambda w, off: pl.BlockSpec((lb, w), lambda bb, h, i: (row(bb, h, i), off + h))
    return pl.pallas_call(
        functools.partial(_gla_kernel, nc=lb // CHUNK),
        name="gla_mixer",
        grid=(b, GLA_HEADS, nb),
        in_specs=[spec(128, OFF_GLA // 128), spec(128, (OFF_GLA + 512) // 128),
                  spec(256, (OFF_GLA + 1024) // 256), spec(256, (OFF_GLA + 2048) // 256),
                  pl.BlockSpec((lb, 128), lambda bb, h, i: (row(bb, h, i), OFF_GALOW // 128)),
                  pl.BlockSpec((128, 128), lambda bb, h, i: (0, h)),
                  pl.BlockSpec((1, 128), lambda bb, h, i: (0, h)),
                  pl.BlockSpec((1, 256), lambda bb, h, i: (0, 0)),
                  pl.BlockSpec((1, 1, GLA_DK, GLA_DV), lambda bb, h, i: (bb, h, 0, 0))],
        out_specs=[pl.BlockSpec((lb, 256), lambda bb, h, i: (row(bb, h, i), h)),
                   pl.BlockSpec((1, 1, GLA_DK, GLA_DV), lambda bb, h, i: (bb, h, 0, 0))],
        out_shape=[jax.ShapeDtypeStruct((b * l, GLA_HEADS * GLA_DV), BF16),
                   jax.ShapeDtypeStruct((b, GLA_HEADS, GLA_DK, GLA_DV), F32)],
        scratch_shapes=[pltpu.VMEM((GLA_DK, GLA_DV), F32)],
        compiler_params=pltpu.CompilerParams(dimension_semantics=("parallel", "parallel", "arbitrary")),
    )(u, u, u, u, u, wa2p, ba, ng, s0)


def _half_sum(x, lo):
    s0 = jnp.sum(jnp.where(lo, x, 0.0), -1, keepdims=True)
    s1 = jnp.sum(jnp.where(lo, 0.0, x), -1, keepdims=True)
    return jnp.where(lo, s0, s1)


def _rwkv_kernel(r_ref, k_ref, v_ref, low_ref, shr_ref, shk_ref, shv_ref, shl_ref,
                 mur_ref, muk_ref, muv_ref, mul_ref, w0_ref, w2_ref, a0_ref, a2_ref, g2_ref,
                 kk_ref, ka_ref, rk_ref, lng_ref, lnb_ref, s0_ref,
                 o_ref, sn_ref, s_ref, sh_ref, *, nc):
    i = pl.program_id(2)
    lb = nc * CHUNK
    hd = RWKV_HEAD

    @pl.when(i == 0)
    def _():
        s_ref[...] = s0_ref[0]
        sh_ref[7:8, 0:128] = shr_ref[0]
        sh_ref[7:8, 128:256] = shk_ref[0]
        sh_ref[7:8, 256:384] = shv_ref[0]
        sh_ref[7:8, 384:640] = shl_ref[0]

    sh_ref[8:8 + lb, 0:128] = r_ref[...]
    sh_ref[8:8 + lb, 128:256] = k_ref[...]
    sh_ref[8:8 + lb, 256:384] = v_ref[...]
    sh_ref[8:8 + lb, 384:640] = low_ref[...]

    tril = _tril_mask(CHUNK)
    stril = _tril_mask(CHUNK, strict=True)
    tril_b = tril.astype(BF16)
    eye_f = (lax.broadcasted_iota(jnp.int32, (CHUNK, CHUNK), 0)
             == lax.broadcasted_iota(jnp.int32, (CHUNK, CHUNK), 1)).astype(F32)
    lo = lax.broadcasted_iota(jnp.int32, (1, 128), 1) < hd

    prep = []
    for c in range(nc):
        cur = sh_ref[8 + c * CHUNK:8 + (c + 1) * CHUNK, :]
        prev = sh_ref[7 + c * CHUNK:7 + (c + 1) * CHUNK, :]
        shift = lambda a, b, mu: a + (b - a) * mu
        r = shift(cur[:, 0:128], prev[:, 0:128], mur_ref[...])
        k = shift(cur[:, 128:256], prev[:, 128:256], muk_ref[...])
        v = shift(cur[:, 256:384], prev[:, 256:384], muv_ref[...])
        low = shift(cur[:, 384:640], prev[:, 384:640], mul_ref[...])
        zw = w0_ref[...] + _mm(jnp.tanh(low[:, 0:64]), w2_ref[...])
        lw = -jnp.exp(-_softplus(-zw) - 0.5)
        a = _sigmoid(a0_ref[...] + _mm(low[:, 64:128], a2_ref[...]))
        gate = _mm(_sigmoid(low[:, 128:256]), g2_ref[...])
        kk = k * kk_ref[...]
        kk = kk * lax.rsqrt(jnp.maximum(_half_sum(kk * kk, lo), 1e-24))
        k2 = k * (1.0 + (a - 1.0) * ka_ref[...])
        lg = _cumsum_rows(lw, tril_b)
        e_neg = jnp.exp(-lg)
        prep.append(dict(a=-kk * jnp.exp(lg - lw), b=kk * a * e_neg, k=k2 * e_neg, r=r * jnp.exp(lg),
                         v=v, gam=jnp.exp(lg[CHUNK - 1:CHUNK, :]), gate=gate,
                         bonus=_half_sum(r * k2 * rk_ref[...], lo) * v))

    units = [(c, hh) for c in range(nc) for hh in range(2)]
    head = lambda name: {(c, hh): prep[c][name][:, hh * hd:(hh + 1) * hd] for c, hh in units}
    at, bt, kt, rt, vv, gam = head("a"), head("b"), head("k"), head("r"), head("v"), head("gam")
    ar = {u: jnp.concatenate([at[u], rt[u]], axis=0) for u in units}
    pb = {u: _mm_nt(ar[u], bt[u]) for u in units}
    pk = {u: _mm_nt(ar[u], kt[u]) for u in units}
    n_ab = {u: jnp.where(stril, pb[u][:CHUNK], 0.0) for u in units}
    m_rb = {u: jnp.where(tril, pb[u][CHUNK:], 0.0) for u in units}
    n_ak = {u: jnp.where(stril, pk[u][:CHUNK], 0.0) for u in units}
    m_rk = {u: jnp.where(tril, pk[u][CHUNK:], 0.0) for u in units}
    t_inv = {u: eye_f + n_ab[u] for u in units}
    pw = n_ab
    for _ in range(5):
        pw = {u: _mm(pw[u], pw[u]) for u in units}
        t_inv = {u: t_inv[u] + _mm(t_inv[u], pw[u]) for u in units}
    akv = {u: _mm(n_ak[u], vv[u]) for u in units}
    a_hat = {u: _mm(t_inv[u], at[u]) for u in units}
    u0 = {u: _mm(t_inv[u], akv[u]) for u in units}
    g_lr = {u: _mm_tn(a_hat[u], bt[u]) * gam[u] for u in units}
    c0 = {u: _mm_tn(jnp.concatenate([u0[u], vv[u]], axis=0),
                    jnp.concatenate([bt[u], kt[u]], axis=0)) * gam[u] for u in units}
    r_hat = {u: rt[u] + _mm(m_rb[u], a_hat[u]) for u in units}
    y0 = {u: _mm(m_rb[u], u0[u]) + _mm(m_rk[u], vv[u]) for u in units}

    s_cur = [s_ref[0], s_ref[1]]
    for c in range(nc):
        ys = []
        for hh in range(2):
            u = (c, hh)
            s0 = s_cur[hh]
            ys.append(_mm_nt(r_hat[u], s0) + y0[u])
            s_cur[hh] = s0 * gam[u] + _mm(s0, g_lr[u]) + c0[u]
        y = jnp.concatenate(ys, axis=1)
        mean = _half_sum(y, lo) * (1.0 / hd)
        yc = y - mean
        var = _half_sum(yc * yc, lo) * (1.0 / hd)
        y = yc * lax.rsqrt(var + RWKV_GN_EPS) * lng_ref[...] + lnb_ref[...]
        o_ref[c * CHUNK:(c + 1) * CHUNK, :] = ((y + prep[c]["bonus"]) * prep[c]["gate"]).astype(o_ref.dtype)
    s_ref[0] = s_cur[0]
    s_ref[1] = s_cur[1]

    sh_ref[7:8, :] = sh_ref[7 + lb:8 + lb, :]

    @pl.when(i == pl.num_programs(2) - 1)
    def _():
        sn_ref[0] = s_ref[...]


def _rwkv_call(u, shift_buf, s0, p, b, l):
    lb = min(l, 1024)
    nb = l // lb
    row = lambda bb, h, i: bb * nb + i
    ublk = lambda off: pl.BlockSpec((lb, 128), lambda bb, h, i: (row(bb, h, i), off // 128 + h))
    sblk = lambda off: pl.BlockSpec((1, 1, 128), lambda bb, h, i: (bb, 0, off // 128 + h))
    vec = lambda: pl.BlockSpec((1, 128), lambda bb, h, i: (0, h))
    const = lambda shape: pl.BlockSpec(shape, lambda bb, h, i: (0,) * len(shape))
    return pl.pallas_call(
        functools.partial(_rwkv_kernel, nc=lb // CHUNK),
        name="rwkv_mixer",
        grid=(b, RWKV_HEADS // 2, nb),
        in_specs=[ublk(OFF_RWKV), ublk(OFF_RWKV + 1024), ublk(OFF_RWKV + 2048),
                  pl.BlockSpec((lb, 256), lambda bb, h, i: (row(bb, h, i), OFF_RLOW // 256)),
                  sblk(0), sblk(1024), sblk(2048),
                  pl.BlockSpec((1, 1, 256), lambda bb, h, i: (bb, 0, 3072 // 256)),
                  vec(), pl.BlockSpec((1, 128), lambda bb, h, i: (0, 8 + h)),
                  pl.BlockSpec((1, 128), lambda bb, h, i: (0, 16 + h)),
                  pl.BlockSpec((1, 256), lambda bb, h, i: (0, 3072 // 256)),
                  vec(), pl.BlockSpec((64, 128), lambda bb, h, i: (0, h)),
                  vec(), pl.BlockSpec((64, 128), lambda bb, h, i: (0, h)),
                  pl.BlockSpec((128, 128), lambda bb, h, i: (0, h)),
                  vec(), vec(), vec(), vec(), vec(),
                  pl.BlockSpec((1, 2, RWKV_HEAD, RWKV_HEAD), lambda bb, h, i: (bb, h, 0, 0))],
        out_specs=[pl.BlockSpec((lb, 128), lambda bb, h, i: (row(bb, h, i), h)),
                   pl.BlockSpec((1, 2, RWKV_HEAD, RWKV_HEAD), lambda bb, h, i: (bb, h, 0, 0))],
        out_shape=[jax.ShapeDtypeStruct((b * l, RWKV_DIM), BF16),
                   jax.ShapeDtypeStruct((b, RWKV_HEADS, RWKV_HEAD, RWKV_HEAD), F32)],
        scratch_shapes=[pltpu.VMEM((2, RWKV_HEAD, RWKV_HEAD), F32),
                        pltpu.VMEM((lb + 8, 640), F32)],
        compiler_params=pltpu.CompilerParams(dimension_semantics=("parallel", "parallel", "arbitrary")),
    )(u, u, u, u, shift_buf, shift_buf, shift_buf, shift_buf,
      p["mu"], p["mu"], p["mu"], p["mu"], p["w0"], p["w2"], p["a0"], p["a2"], p["g2"],
      p["k_k"], p["k_a"], p["r_k"], p["ln_g"], p["ln_b"], s0)


def _ssd_kernel(z_ref, x_ref, bm_ref, cm_ref, dt_ref, cbx_ref, cbb_ref, cbc_ref,
                cwx_ref, cwb_ref, cwc_ref, cbiasx_ref, cbiasb_ref, cbiasc_ref,
                dtb_ref, alog_ref, dsk_ref, ng_ref, s0_ref,
                o_ref, sn_ref, s_ref, full_ref, *, nc):
    i = pl.program_id(2)
    lb = nc * CHUNK
    hpg = SSM_HEADS // SSM_GROUPS
    p = SSM_HEADDIM
    wx = hpg * p

    @pl.when(i == 0)
    def _():
        s_ref[...] = s0_ref[0].reshape(hpg * p, SSM_STATE)
        full_ref[5:8, 0:wx] = cbx_ref[0]
        full_ref[5:8, wx:wx + 128] = cbb_ref[0]
        full_ref[5:8, wx + 128:wx + 256] = cbc_ref[0]

    rnd = lambda a: a.astype(BF16).astype(F32)

    @pl.when(i == 0)
    def _():
        full_ref[5:8, :] = rnd(full_ref[5:8, :])
        full_ref[8 + lb:16 + lb, :] = jnp.zeros((8, wx + 256), F32)
        full_ref[0:5, :] = jnp.zeros((5, wx + 256), F32)

    full_ref[8:8 + lb, 0:wx] = rnd(x_ref[...])
    full_ref[8:8 + lb, wx:wx + 128] = rnd(bm_ref[...])
    full_ref[8:8 + lb, wx + 128:wx + 256] = rnd(cm_ref[...])

    tril = _tril_mask(CHUNK)
    tril_b = tril.astype(BF16)
    cw = jnp.concatenate([cwx_ref[...], cwb_ref[...], cwc_ref[...]], axis=1)
    cbias = jnp.concatenate([cbiasx_ref[...], cbiasb_ref[...], cbiasc_ref[...]], axis=1)
    a_row = -jnp.exp(alog_ref[0])

    iota = lambda shape, d: lax.broadcasted_iota(jnp.int32, shape, d)
    head_of = lambda idx: lax.shift_right_logical(idx, jnp.int32(6))
    expand = (iota((128, wx), 0) == head_of(iota((128, wx), 1))).astype(BF16)
    pos = jnp.bitwise_and(iota((CHUNK, wx), 1), jnp.int32(CHUNK - 1))
    tril_t = iota((CHUNK, wx), 0) >= pos
    diag_t = iota((CHUNK, wx), 0) == pos
    blockdiag = head_of(iota((wx, wx), 0)) == head_of(iota((wx, wx), 1))
    ones_b = jnp.ones((CHUNK, CHUNK), BF16)

    def split3(x):
        x1 = x.astype(BF16)
        r1 = x - x1.astype(F32)
        x2 = r1.astype(BF16)
        return x1, x2, (r1 - x2.astype(F32)).astype(BF16)

    dot = lambda a, b: jnp.dot(a, b, preferred_element_type=F32)
    per_head = lambda x: sum(dot(piece, expand) for piece in split3(x))

    chunks = range(nc)
    xs, bm, cm, dtv, acum = [], [], [], [], []
    win = CHUNK + 16
    shifts = [(iota((CHUNK, win), 1) == iota((CHUNK, win), 0) + jnp.int32(5 + j)).astype(BF16)
              for j in range(SSM_CONV - 1)]
    for c in chunks:
        base = c * CHUNK
        window = full_ref[base:base + win, :].astype(BF16)
        conv = cbias + cw[SSM_CONV - 1:SSM_CONV, :] * full_ref[8 + base:8 + base + CHUNK, :]
        for j in range(SSM_CONV - 1):
            conv = conv + cw[j:j + 1, :] * dot(shifts[j], window)
        conv = _silu(conv)
        xs.append(conv[:, 0:wx])
        bm.append(conv[:, wx:wx + 128])
        cm.append(conv[:, wx + 128:wx + 256])
        dtv.append(_softplus(dt_ref[base:base + CHUNK, :] + dtb_ref[0]))
    acum = [_cumsum_rows(dtv[c] * a_row, tril_b) for c in chunks]
    a_col = [per_head(acum[c]) for c in chunks]
    a_row_e = [sum(dot(ones_b, piece) for piece in split3(jnp.where(diag_t, a_col[c], 0.0)))
               for c in chunks]
    dt_e = [per_head(dtv[c]) for c in chunks]
    cb_t = [_mm_nt(cm[c], jnp.concatenate([bm[c]] * hpg, axis=0)) for c in chunks]
    dec = [jnp.where(tril_t, jnp.exp(jnp.where(tril_t, a_col[c] - a_row_e[c], 0.0)), 0.0) for c in chunks]
    xdt = [xs[c] * dt_e[c] for c in chunks]
    xdt_bd = [jnp.where(blockdiag, jnp.concatenate([xdt[c].astype(BF16)] * hpg, axis=0),
                        jnp.zeros((), BF16)) for c in chunks]
    y_diag = [dot((cb_t[c] * dec[c]).astype(BF16), xdt_bd[c]) for c in chunks]
    a_last = [a_col[c][CHUNK - 1:CHUNK, :] for c in chunks]
    st = [_mm_tn(xdt[c] * jnp.exp(a_last[c] - a_col[c]), bm[c]) for c in chunks]
    e_end = [jnp.exp(acum[c][CHUNK - 1:CHUNK, :]) for c in chunks]
    e_in = [jnp.exp(a_col[c]) for c in chunks]
    s_cur = s_ref[...]
    for c in chunks:
        base = c * CHUNK
        y = y_diag[c] + _mm_nt(cm[c], s_cur) * e_in[c] + xs[c] * dsk_ref[...]
        s_cur = jnp.concatenate([s_cur[h * p:(h + 1) * p, :] * e_end[c][:, h:h + 1] for h in range(hpg)],
                                axis=0) + st[c]
        y = y * _silu(z_ref[base:base + CHUNK, :])
        y = y * lax.rsqrt(jnp.mean(y * y, -1, keepdims=True) + SSM_NORM_EPS) * ng_ref[...]
        o_ref[base:base + CHUNK, :] = y.astype(o_ref.dtype)
    s_ref[...] = s_cur

    full_ref[5:8, :] = full_ref[5 + lb:8 + lb, :]

    @pl.when(i == pl.num_programs(2) - 1)
    def _():
        sn_ref[0] = s_ref[...].reshape(hpg, p, SSM_STATE)


def _ssd_call(u, conv_buf, s0, p, b, l):
    lb = min(l, 256)
    nb = l // lb
    hpg = SSM_HEADS // SSM_GROUPS
    row = lambda bb, g, i: bb * nb + i
    ublk = lambda w, off: pl.BlockSpec((lb, w), lambda bb, g, i: (row(bb, g, i), off // w + g))
    gvec = lambda w, off: pl.BlockSpec((1, w), lambda bb, g, i: (0, off // w + g))
    gtap = lambda w, off: pl.BlockSpec((SSM_CONV, w), lambda bb, g, i: (0, off // w + g))
    gbuf = lambda w, off: pl.BlockSpec((1, SSM_CONV - 1, w), lambda bb, g, i: (bb, 0, off // w + g))
    return pl.pallas_call(
        functools.partial(_ssd_kernel, nc=lb // CHUNK),
        name="ssd_mixer",
        grid=(b, SSM_GROUPS, nb),
        in_specs=[ublk(512, OFF_SSM), ublk(512, OFF_SSM + 2048),
                  ublk(128, OFF_SSM + 4096), ublk(128, OFF_SSM + 4608), ublk(128, OFF_DT),
                  gbuf(512, 0), gbuf(128, 2048), gbuf(128, 2560),
                  gtap(512, 0), gtap(128, 2048), gtap(128, 2560),
                  gvec(512, 0), gvec(128, 2048), gvec(128, 2560),
                  pl.BlockSpec((1, 1, 128), lambda bb, g, i: (g, 0, 0)),
                  pl.BlockSpec((1, 1, 128), lambda bb, g, i: (g, 0, 0)),
                  gvec(512, 0), gvec(512, 0),
                  pl.BlockSpec((1, hpg, SSM_HEADDIM, SSM_STATE), lambda bb, g, i: (bb, g, 0, 0))],
        out_specs=[pl.BlockSpec((lb, 512), lambda bb, g, i: (row(bb, g, i), g)),
                   pl.BlockSpec((1, hpg, SSM_HEADDIM, SSM_STATE), lambda bb, g, i: (bb, g, 0, 0))],
        out_shape=[jax.ShapeDtypeStruct((b * l, SSM_INNER), BF16),
                   jax.ShapeDtypeStruct((b, SSM_HEADS, SSM_HEADDIM, SSM_STATE), F32)],
        scratch_shapes=[pltpu.VMEM((hpg * SSM_HEADDIM, SSM_STATE), F32),
                        pltpu.VMEM((lb + 16, 768), F32)],
        compiler_params=pltpu.CompilerParams(dimension_semantics=("parallel", "parallel", "arbitrary")),
    )(u, u, u, u, u, conv_buf, conv_buf, conv_buf,
      p["conv_w"], p["conv_w"], p["conv_w"], p["conv_b"], p["conv_b"], p["conv_b"],
      p["dt_bias"], p["a_log"], p["d_skip"], p["norm_g"], s0)


def _merge_kernel(gate_ref, og_ref, or_ref, os_ref, x_ref, wg_ref, wr_ref, ws_ref, wo_ref,
                  g_ref, b_ref, o_ref):
    d = D_MODEL
    m = _sigmoid(gate_ref[:, 0:d]) * jnp.dot(og_ref[...], wg_ref[...], preferred_element_type=F32)
    m = m + _sigmoid(gate_ref[:, d:2 * d]) * jnp.dot(or_ref[...], wr_ref[...], preferred_element_type=F32)
    m = m + _sigmoid(gate_ref[:, 2 * d:3 * d]) * jnp.dot(os_ref[...], ws_ref[...], preferred_element_type=F32)
    y = DN_ALPHA * x_ref[...] + _mm(m, wo_ref[...])
    o_ref[...] = _layer_norm(y, g_ref[...], b_ref[...])


def _merge_call(u, og, orr, os_, x, wg, wr, ws, wo, g, b):
    t = x.shape[0]
    tm = min(t, 256)
    tok = lambda w: pl.BlockSpec((tm, w), lambda i: (i, 0))
    const = lambda r, c: pl.BlockSpec((r, c), lambda i: (0, 0))
    return pl.pallas_call(
        _merge_kernel,
        name="merge_outproj_ln",
        grid=(t // tm,),
        in_specs=[tok(3 * D_MODEL), tok(D_MODEL), tok(D_MODEL), tok(SSM_INNER), tok(D_MODEL),
                  const(D_MODEL, D_MODEL), const(D_MODEL, D_MODEL), const(SSM_INNER, D_MODEL),
                  const(D_MODEL, D_MODEL), const(1, D_MODEL), const(1, D_MODEL)],
        out_specs=tok(D_MODEL),
        out_shape=jax.ShapeDtypeStruct((t, D_MODEL), F32),
        compiler_params=pltpu.CompilerParams(dimension_semantics=("parallel",),
                                             vmem_limit_bytes=VMEM_LIMIT),
    )(u, og, orr, os_, x, wg, wr, ws, wo, g, b)


def _router_gates(xb, wrt_ref, rb_ref, key_ref):
    tm = xb.shape[0]
    per_group = N_EXPERTS // N_EXPERT_GROUPS
    s_t = _sigmoid(_mm_nt(wrt_ref[...], xb))
    sb = s_t + rb_ref[...]
    gscore = []
    for g in range(N_EXPERT_GROUPS):
        xg = sb[g * per_group:(g + 1) * per_group, :]
        m1 = jnp.max(xg, axis=0, keepdims=True)
        eq = xg == m1
        cnt = jnp.sum(eq.astype(F32), axis=0, keepdims=True)
        m2 = jnp.max(jnp.where(eq, -jnp.inf, xg), axis=0, keepdims=True)
        gscore.append(m1 + jnp.where(cnt >= 2.0, m1, m2))
    for g in range(N_EXPERT_GROUPS):
        rank = jnp.zeros((1, tm), F32)
        for g2 in range(N_EXPERT_GROUPS):
            if g2 == g:
                continue
            beats = (gscore[g2] >= gscore[g]) if g2 < g else (gscore[g2] > gscore[g])
            rank = rank + beats.astype(F32)
        keep = rank < float(TOPK_GROUPS)
        key_ref[g * per_group:(g + 1) * per_group, :] = jnp.where(
            keep, sb[g * per_group:(g + 1) * per_group, :], -jnp.inf)
    key = key_ref[...]
    eidx = lax.broadcasted_iota(jnp.int32, (N_EXPERTS, tm), 0)

    def body(ep, rank):
        rowv = key_ref[pl.ds(ep, 1), :]
        beats = (rowv > key) | ((rowv == key) & (ep < eidx))
        return rank + beats.astype(F32)

    rank = lax.fori_loop(0, N_EXPERTS, body, jnp.zeros((N_EXPERTS, tm), F32))
    w = jnp.where(rank < float(TOP_K), s_t, 0.0)
    return w / jnp.sum(w, axis=0, keepdims=True) * ROUTED_SCALE


def _moe_kernel(x_ref, wrt_ref, rb_ref, w1_ref, w3_ref, w2_ref, s1_ref, s3_ref, s2_ref,
                g_ref, b_ref, o_ref, xb_ref, gates_ref, key_ref, *, sub):
    e = pl.program_id(1)
    tm = x_ref.shape[0]

    @pl.when(e == 0)
    def _():
        xb_ref[...] = x_ref[...].astype(BF16)
        g_t = _router_gates(xb_ref[...], wrt_ref, rb_ref, key_ref)
        gates_ref[...] = jnp.transpose(jnp.concatenate([g_t, jnp.zeros_like(g_t)], axis=0))
        for s in range(tm // sub):
            rows = pl.ds(s * sub, sub)
            xs = xb_ref[rows, :]
            hid = _silu(jnp.dot(xs, s1_ref[...], preferred_element_type=F32)) * jnp.dot(
                xs, s3_ref[...], preferred_element_type=F32)
            o_ref[rows, :] = _mm(hid, s2_ref[...])

    lane = lax.broadcasted_iota(jnp.int32, (1, 128), 1)
    per_step = w1_ref.shape[0]
    for s in range(tm // sub):
        rows = pl.ds(s * sub, sub)
        xs = xb_ref[rows, :]
        acc = o_ref[rows, :]
        for j in range(per_step):
            gcol = jnp.sum(jnp.where(lane == e * per_step + j, gates_ref[rows, :], 0.0),
                           axis=1, keepdims=True)
            hid = _silu(jnp.dot(xs, w1_ref[j], preferred_element_type=F32)) * jnp.dot(
                xs, w3_ref[j], preferred_element_type=F32)
            acc = acc + _mm(hid * gcol, w2_ref[j])
        o_ref[rows, :] = acc

    @pl.when(e == pl.num_programs(1) - 1)
    def _():
        o_ref[...] = _layer_norm(DN_ALPHA * x_ref[...] + o_ref[...], g_ref[...], b_ref[...])


def _moe_call(h, wrt, rb, w1, w3, w2, s1, s3, s2, g, b):
    t = h.shape[0]
    tm = min(t, 2048)
    sub = min(tm, 512)
    per_step = 2
    const = lambda r, c: pl.BlockSpec((r, c), lambda i, e: (0, 0))
    return pl.pallas_call(
        functools.partial(_moe_kernel, sub=sub),
        name="moe_ln",
        grid=(t // tm, N_EXPERTS // per_step),
        in_specs=[pl.BlockSpec((tm, D_MODEL), lambda i, e: (i, 0)),
                  const(N_EXPERTS, D_MODEL), const(N_EXPERTS, 1),
                  pl.BlockSpec((per_step, D_MODEL, D_EXPERT), lambda i, e: (e, 0, 0)),
                  pl.BlockSpec((per_step, D_MODEL, D_EXPERT), lambda i, e: (e, 0, 0)),
                  pl.BlockSpec((per_step, D_EXPERT, D_MODEL), lambda i, e: (e, 0, 0)),
                  const(D_MODEL, D_EXPERT), const(D_MODEL, D_EXPERT), const(D_EXPERT, D_MODEL),
                  const(1, D_MODEL), const(1, D_MODEL)],
        out_specs=pl.BlockSpec((tm, D_MODEL), lambda i, e: (i, 0)),
        out_shape=jax.ShapeDtypeStruct((t, D_MODEL), F32),
        scratch_shapes=[pltpu.VMEM((tm, D_MODEL), BF16),
                        pltpu.VMEM((tm, 128), F32),
                        pltpu.VMEM((N_EXPERTS, tm), F32)],
        compiler_params=pltpu.CompilerParams(dimension_semantics=("parallel", "arbitrary"),
                                             vmem_limit_bytes=VMEM_LIMIT),
    )(h, wrt, rb, w1, w3, w2, s1, s3, s2, g, b)


def _prep_layer(w_in, gla_wa2, gla_ba, gla_norm_g, gla_wo,
                rwkv_mu, rwkv_w0, rwkv_w2, rwkv_a0, rwkv_a2, rwkv_g2, rwkv_k_k, rwkv_k_a, rwkv_r_k,
                rwkv_ln_g, rwkv_ln_b, rwkv_wo,
                ssm_conv_w, ssm_conv_b, ssm_dt_bias, ssm_a_log, ssm_d, ssm_norm_g, ssm_wo,
                w_out, ln1_g, ln1_b,
                w_router, router_bias, exp_w1, exp_w3, exp_w2, sh_w1, sh_w3, sh_w2, ln2_g, ln2_b):
    d = D_MODEL
    o_gla, o_rwkv = 3 * d, 3 * d + 3088
    o_ssm = o_rwkv + RWKV_PROJ
    zeros = lambda n: jnp.zeros((d, n), w_in.dtype)
    dt0 = o_ssm + SSM_INNER + SSM_CONV_DIM
    hpg = SSM_HEADS // SSM_GROUPS
    dt_cols = []
    for g in range(SSM_GROUPS):
        dt_cols += [w_in[:, dt0 + g * hpg:dt0 + (g + 1) * hpg], zeros(128 - hpg)]
    w_u = jnp.concatenate(
        [w_in[:, 0:3 * d], w_in[:, o_gla:o_gla + 3072], w_in[:, o_rwkv:o_rwkv + 3072],
         w_in[:, o_ssm:o_ssm + SSM_INNER + SSM_CONV_DIM], w_in[:, o_rwkv + 3072:o_rwkv + 3328],
         w_in[:, o_gla + 3072:o_gla + 3088], zeros(128 - GLA_RANK)] + dt_cols + [zeros(128)],
        axis=1).astype(BF16)
    assert w_u.shape[1] == N_U
    row = lambda v: v.reshape(1, -1)
    pad_groups = lambda v: jnp.pad(v.reshape(SSM_GROUPS, 1, hpg), ((0, 0), (0, 0), (0, 128 - hpg)))
    return dict(
        w_u=w_u,
        gla=dict(wa2=jnp.pad(gla_wa2, ((0, 128 - GLA_RANK), (0, 0))).astype(BF16),
                 ba=row(gla_ba), ng=row(gla_norm_g)),
        rwkv=dict(mu=row(rwkv_mu), w0=row(rwkv_w0), w2=rwkv_w2.astype(BF16), a0=row(rwkv_a0),
                  a2=rwkv_a2.astype(BF16), g2=rwkv_g2.astype(BF16), k_k=row(rwkv_k_k),
                  k_a=row(rwkv_k_a), r_k=row(rwkv_r_k), ln_g=row(rwkv_ln_g), ln_b=row(rwkv_ln_b)),
        ssm=dict(conv_w=ssm_conv_w, conv_b=row(ssm_conv_b), dt_bias=pad_groups(ssm_dt_bias),
                 a_log=pad_groups(ssm_a_log), d_skip=row(jnp.repeat(ssm_d, SSM_HEADDIM)),
                 norm_g=row(ssm_norm_g)),
        merge=(gla_wo.astype(BF16), rwkv_wo.astype(BF16), ssm_wo.astype(BF16), w_out.astype(BF16),
               row(ln1_g), row(ln1_b)),
        moe=(jnp.transpose(w_router).astype(BF16), router_bias.reshape(-1, 1),
             exp_w1.astype(BF16), exp_w3.astype(BF16), exp_w2.astype(BF16),
             sh_w1.astype(BF16), sh_w3.astype(BF16), sh_w2.astype(BF16), row(ln2_g), row(ln2_b)),
    )


def _layer(h, st_gla, st_rwkv, st_shift, st_ssm, st_conv, lp, b, l):
    u = _inproj_call(h, lp["w_u"])
    o_gla, s_gla = _gla_call(u, st_gla, lp["gla"]["wa2"], lp["gla"]["ba"], lp["gla"]["ng"], b, l)
    o_rwkv, s_rwkv = _rwkv_call(u, st_shift, st_rwkv, lp["rwkv"], b, l)
    o_ssm, s_ssm = _ssd_call(u, st_conv, st_ssm, lp["ssm"], b, l)
    h1 = _merge_call(u, o_gla, o_rwkv, o_ssm, h, *lp["merge"])
    h2 = _moe_call(h1, *lp["moe"])
    u3 = u.reshape(b, l, N_U)
    shift_new = jnp.concatenate([u3[:, l - 1:, OFF_RWKV:OFF_RWKV + 3072],
                                 u3[:, l - 1:, OFF_RLOW:OFF_RLOW + 256]], axis=-1)
    conv_new = u3[:, l - (SSM_CONV - 1):, OFF_SSM + SSM_INNER:OFF_SSM + SSM_INNER + SSM_CONV_DIM]
    return h2, s_gla, s_rwkv, shift_new, s_ssm, conv_new


def _zero_states(batch):
    return (jnp.zeros((batch, GLA_HEADS, GLA_DK, GLA_DV), F32),
            jnp.zeros((batch, RWKV_HEADS, RWKV_HEAD, RWKV_HEAD), F32),
            jnp.zeros((batch, 1, RWKV_PROJ), F32),
            jnp.zeros((batch, SSM_HEADS, SSM_HEADDIM, SSM_STATE), F32),
            jnp.zeros((batch, SSM_CONV - 1, SSM_CONV_DIM), F32))


def kernel(x_prompt, x_sample, state_gla, state_rwkv, state_rwkv_shift, state_ssm, state_ssm_conv,
           ln_in_g, ln_in_b, w_in, gla_wa2, gla_ba, gla_norm_g, gla_wo,
           rwkv_mu, rwkv_w0, rwkv_w2, rwkv_a0, rwkv_a2, rwkv_g2, rwkv_k_k, rwkv_k_a, rwkv_r_k,
           rwkv_ln_g, rwkv_ln_b, rwkv_wo,
           ssm_conv_w, ssm_conv_b, ssm_dt_bias, ssm_a_log, ssm_d, ssm_norm_g, ssm_wo,
           w_out, ln1_g, ln1_b,
           w_router, router_bias, exp_w1, exp_w3, exp_w2, sh_w1, sh_w3, sh_w2, ln2_g, ln2_b):
    layer_params = (w_in, gla_wa2, gla_ba, gla_norm_g, gla_wo,
                    rwkv_mu, rwkv_w0, rwkv_w2, rwkv_a0, rwkv_a2, rwkv_g2, rwkv_k_k, rwkv_k_a, rwkv_r_k,
                    rwkv_ln_g, rwkv_ln_b, rwkv_wo,
                    ssm_conv_w, ssm_conv_b, ssm_dt_bias, ssm_a_log, ssm_d, ssm_norm_g, ssm_wo,
                    w_out, ln1_g, ln1_b,
                    w_router, router_bias, exp_w1, exp_w3, exp_w2, sh_w1, sh_w3, sh_w2, ln2_g, ln2_b)
    depth = w_in.shape[0]
    bp, lp_, d = x_prompt.shape
    bs, ls, _ = x_sample.shape
    h_p = _ln_call(x_prompt.reshape(bp * lp_, d), ln_in_g, ln_in_b)
    h_s = _ln_call(x_sample.reshape(bs * ls, d), ln_in_g, ln_in_b)
    new_p, new_s = [], []
    for i in range(depth):
        lp = _prep_layer(*[t[i] for t in layer_params])
        h_p, *st_p = _layer(h_p, *_zero_states(bp), lp, bp, lp_)
        new_p.append(st_p)
        h_s, *st_s = _layer(h_s, state_gla[i], state_rwkv[i], state_rwkv_shift[i], state_ssm[i],
                            state_ssm_conv[i], lp, bs, ls)
        new_s.append(st_s)
    stack = lambda rows: tuple(jnp.stack(z) for z in zip(*rows))
    return (h_p.reshape(bp, lp_, d), h_s.reshape(bs, ls, d)) + stack(new_p) + stack(new_s)
```

```python
import functools

import jax
import jax.numpy as jnp
from jax import lax
from jax.experimental import pallas as pl
from jax.experimental.pallas import tpu as pltpu

F32 = jnp.float32
BF16 = jnp.bfloat16

D_MODEL = 1024
CHUNK = 64
GLA_HEADS, GLA_DK, GLA_DV, GLA_RANK = 4, 128, 256, 16
GLA_GATE_NORM = 16.0
GLA_NORM_EPS = 1e-5
RWKV_HEAD, RWKV_HEADS, RWKV_DIM = 64, 16, 1024
RWKV_PROJ = 3 * RWKV_DIM + 64 + 64 + 128
RWKV_GN_EPS = 64e-5
SSM_INNER, SSM_HEADDIM, SSM_HEADS, SSM_GROUPS, SSM_STATE, SSM_CONV = 2048, 64, 32, 4, 128, 4
SSM_CONV_DIM = SSM_INNER + 2 * SSM_GROUPS * SSM_STATE
SSM_NORM_EPS = 1e-5
N_EXPERTS, TOP_K, N_EXPERT_GROUPS, TOPK_GROUPS, D_EXPERT = 64, 8, 8, 4, 256
ROUTED_SCALE = 2.5
DEPTH = 2
DN_ALPHA = (2.0 * DEPTH) ** 0.25
LN_EPS = 1e-5

OFF_GATE, OFF_GLA, OFF_RWKV, OFF_SSM = 0, 3072, 6144, 9216
OFF_RLOW, OFF_GALOW, OFF_DT = 14336, 14592, 14720
N_U = 15360
VMEM_LIMIT = 56 * 1024 * 1024


def _mm(a, b):
    return jnp.dot(a.astype(BF16), b.astype(BF16), preferred_element_type=F32)


def _mm_nt(a, b):
    return lax.dot_general(a.astype(BF16), b.astype(BF16), (((1,), (1,)), ((), ())),
                           preferred_element_type=F32)


def _mm_tn(a, b):
    return lax.dot_general(a.astype(BF16), b.astype(BF16), (((0,), (0,)), ((), ())),
                           preferred_element_type=F32)


def _sigmoid(x):
    return 1.0 / (1.0 + jnp.exp(-x))


def _silu(x):
    return x * _sigmoid(x)


def _softplus(x):
    return jnp.maximum(x, 0.0) + jnp.log(1.0 + jnp.exp(-jnp.abs(x)))


def _layer_norm(x, g, b):
    mu = jnp.mean(x, -1, keepdims=True)
    xc = x - mu
    var = jnp.mean(xc * xc, -1, keepdims=True)
    return xc * lax.rsqrt(var + LN_EPS) * g + b


def _tril_mask(n, strict=False):
    r = lax.broadcasted_iota(jnp.int32, (n, n), 0)
    c = lax.broadcasted_iota(jnp.int32, (n, n), 1)
    return (r > c) if strict else (r >= c)


def _cumsum_rows(x, tril_bf16):
    x1 = x.astype(BF16)
    r1 = x - x1.astype(F32)
    x2 = r1.astype(BF16)
    x3 = (r1 - x2.astype(F32)).astype(BF16)
    dot = lambda p: jnp.dot(tril_bf16, p, preferred_element_type=F32)
    return dot(x1) + dot(x2) + dot(x3)


def _ln_kernel(x_ref, g_ref, b_ref, o_ref):
    o_ref[...] = _layer_norm(x_ref[...], g_ref[...], b_ref[...])


def _ln_call(x, g, b):
    t = x.shape[0]
    tm = min(t, 512)
    return pl.pallas_call(
        _ln_kernel,
        name="ln_in",
        grid=(t // tm,),
        in_specs=[pl.BlockSpec((tm, D_MODEL), lambda i: (i, 0)),
                  pl.BlockSpec((1, D_MODEL), lambda i: (0, 0)),
                  pl.BlockSpec((1, D_MODEL), lambda i: (0, 0))],
        out_specs=pl.BlockSpec((tm, D_MODEL), lambda i: (i, 0)),
        out_shape=jax.ShapeDtypeStruct((t, D_MODEL), F32),
        compiler_params=pltpu.CompilerParams(dimension_semantics=("parallel",)),
    )(x, g.reshape(1, -1), b.reshape(1, -1))


def _inproj_kernel(x_ref, w_ref, o_ref, xb_ref):
    @pl.when(pl.program_id(1) == 0)
    def _():
        xb_ref[...] = x_ref[...].astype(BF16)

    o_ref[...] = jnp.dot(xb_ref[...], w_ref[...], preferred_element_type=F32)


def _inproj_call(h, w):
    t, n = h.shape[0], w.shape[1]
    tm = min(t, 1024)
    tn = 1024
    return pl.pallas_call(
        _inproj_kernel,
        name="inproj",
        grid=(t // tm, n // tn),
        in_specs=[pl.BlockSpec((tm, D_MODEL), lambda i, j: (i, 0)),
                  pl.BlockSpec((D_MODEL, tn), lambda i, j: (0, j))],
        out_specs=pl.BlockSpec((tm, tn), lambda i, j: (i, j)),
        out_shape=jax.ShapeDtypeStruct((t, n), F32),
        scratch_shapes=[pltpu.VMEM((tm, D_MODEL), BF16)],
        compiler_params=pltpu.CompilerParams(dimension_semantics=("parallel", "arbitrary")),
    )(h, w)


def _gla_kernel(q_ref, k_ref, v_ref, r_ref, al_ref, wa2_ref, ba_ref, ng_ref, s0_ref,
                o_ref, sn_ref, s_ref, *, nc):
    i = pl.program_id(2)

    @pl.when(i == 0)
    def _():
        s_ref[...] = s0_ref[0, 0]

    tril = _tril_mask(CHUNK)
    tril_b = tril.astype(BF16)
    eye = (lax.broadcasted_iota(jnp.int32, (GLA_DK, GLA_DK), 0)
           == lax.broadcasted_iota(jnp.int32, (GLA_DK, GLA_DK), 1))
    chunks = range(nc)
    sl = [pl.ds(c * CHUNK, CHUNK) for c in chunks]
    z = _mm(al_ref[...], wa2_ref[...]) + ba_ref[...]
    la = -_softplus(-z) * (1.0 / GLA_GATE_NORM)
    g = [_cumsum_rows(la[c * CHUNK:(c + 1) * CHUNK], tril_b) for c in chunks]
    g_last = [g[c][CHUNK - 1:CHUNK, :] for c in chunks]
    q_dec = [q_ref[sl[c], :] * jnp.exp(g[c]) * (GLA_DK ** -0.5) for c in chunks]
    k_inv = [k_ref[sl[c], :] * jnp.exp(-g[c]) for c in chunks]
    k_end = [k_ref[sl[c], :] * jnp.exp(g_last[c] - g[c]) for c in chunks]
    scores = [jnp.where(tril, _mm_nt(q_dec[c], k_inv[c]), 0.0) for c in chunks]
    kv = [_mm_tn(k_end[c], v_ref[sl[c], :]) for c in chunks]
    o_intra = [_mm(scores[c], v_ref[sl[c], :]) for c in chunks]
    dec_col = [jnp.sum(jnp.where(eye, jnp.exp(g_last[c]), 0.0), axis=1, keepdims=True) for c in chunks]
    s_cur = s_ref[...]
    for c in chunks:
        o = o_intra[c] + _mm(q_dec[c], s_cur)
        s_cur = s_cur * dec_col[c] + kv[c]
        o = o * lax.rsqrt(jnp.mean(o * o, -1, keepdims=True) + GLA_NORM_EPS) * ng_ref[...]
        o_ref[sl[c], :] = (o * _silu(r_ref[sl[c], :])).astype(o_ref.dtype)
    s_ref[...] = s_cur

    @pl.when(i == pl.num_programs(2) - 1)
    def _():
        sn_ref[0, 0] = s_ref[...]


def _gla_call(u, s0, wa2p, ba, ng, b, l):
    lb = min(l, 512)
    nb = l // lb
    row = lambda bb, h, i: bb * nb + i
    spec = lambda w, off: pl.BlockSpec((lb, w), lambda bb, h, i: (row(bb, h, i), off + h))
    return pl.pallas_call(
        functools.partial(_gla_kernel, nc=lb // CHUNK),
        name="gla_mixer",
        grid=(b, GLA_HEADS, nb),
        in_specs=[spec(128, OFF_GLA // 128), spec(128, (OFF_GLA + 512) // 128),
                  spec(256, (OFF_GLA + 1024) // 256), spec(256, (OFF_GLA + 2048) // 256),
                  pl.BlockSpec((lb, 128), lambda bb, h, i: (row(bb, h, i), OFF_GALOW // 128)),
                  pl.BlockSpec((128, 128), lambda bb, h, i: (0, h)),
                  pl.BlockSpec((1, 128), lambda bb, h, i: (0, h)),
                  pl.BlockSpec((1, 256), lambda bb, h, i: (0, 0)),
                  pl.BlockSpec((1, 1, GLA_DK, GLA_DV), lambda bb, h, i: (bb, h, 0, 0))],
        out_specs=[pl.BlockSpec((lb, 256), lambda bb, h, i: (row(bb, h, i), h)),
                   pl.BlockSpec((1, 1, GLA_DK, GLA_DV), lambda bb, h, i: (bb, h, 0, 0))],
        out_shape=[jax.ShapeDtypeStruct((b * l, GLA_HEADS * GLA_DV), BF16),
                   jax.ShapeDtypeStruct((b, GLA_HEADS, GLA_DK, GLA_DV), F32)],
        scratch_shapes=[pltpu.VMEM((GLA_DK, GLA_DV), F32)],
        compiler_params=pltpu.CompilerParams(dimension_semantics=("parallel", "parallel", "arbitrary")),
    )(u, u, u, u, u, wa2p, ba, ng, s0)


def _half_sum(x, lo):
    s0 = jnp.sum(jnp.where(lo, x, 0.0), -1, keepdims=True)
    s1 = jnp.sum(jnp.where(lo, 0.0, x), -1, keepdims=True)
    return jnp.where(lo, s0, s1)


def _rwkv_kernel(r_ref, k_ref, v_ref, low_ref, shr_ref, shk_ref, shv_ref, shl_ref,
                 mur_ref, muk_ref, muv_ref, mul_ref, w0_ref, w2_ref, a0_ref, a2_ref, g2_ref,
                 kk_ref, ka_ref, rk_ref, lng_ref, lnb_ref, s0_ref,
                 o_ref, sn_ref, s_ref, sh_ref, *, nc):
    i = pl.program_id(2)
    lb = nc * CHUNK
    hd = RWKV_HEAD

    @pl.when(i == 0)
    def _():
        zero = jnp.zeros((hd, hd), F32)
        s_ref[...] = jnp.concatenate([jnp.concatenate([s0_ref[0, 0], zero], axis=1),
                                      jnp.concatenate([zero, s0_ref[0, 1]], axis=1)], axis=0)
        sh_ref[7:8, 0:128] = shr_ref[0]
        sh_ref[7:8, 128:256] = shk_ref[0]
        sh_ref[7:8, 256:384] = shv_ref[0]
        sh_ref[7:8, 384:640] = shl_ref[0]

    sh_ref[8:8 + lb, 0:128] = r_ref[...]
    sh_ref[8:8 + lb, 128:256] = k_ref[...]
    sh_ref[8:8 + lb, 256:384] = v_ref[...]
    sh_ref[8:8 + lb, 384:640] = low_ref[...]

    tril_b = _tril_mask(CHUNK).astype(BF16)
    row = lax.broadcasted_iota(jnp.int32, (CHUNK, 128), 0)
    pos = jnp.bitwise_and(lax.broadcasted_iota(jnp.int32, (CHUNK, 128), 1), jnp.int32(hd - 1))
    tril, stril, eye_f = row >= pos, row > pos, (row == pos).astype(F32)
    lo = lax.broadcasted_iota(jnp.int32, (1, 128), 1) < hd

    prep = []
    for c in range(nc):
        cur = sh_ref[8 + c * CHUNK:8 + (c + 1) * CHUNK, :]
        prev = sh_ref[7 + c * CHUNK:7 + (c + 1) * CHUNK, :]
        shift = lambda a, b, mu: a + (b - a) * mu
        r = shift(cur[:, 0:128], prev[:, 0:128], mur_ref[...])
        k = shift(cur[:, 128:256], prev[:, 128:256], muk_ref[...])
        v = shift(cur[:, 256:384], prev[:, 256:384], muv_ref[...])
        low = shift(cur[:, 384:640], prev[:, 384:640], mul_ref[...])
        zw = w0_ref[...] + _mm(jnp.tanh(low[:, 0:64]), w2_ref[...])
        lw = -jnp.exp(-_softplus(-zw) - 0.5)
        a = _sigmoid(a0_ref[...] + _mm(low[:, 64:128], a2_ref[...]))
        gate = _mm(_sigmoid(low[:, 128:256]), g2_ref[...])
        kk = k * kk_ref[...]
        kk = kk * lax.rsqrt(jnp.maximum(_half_sum(kk * kk, lo), 1e-24))
        k2 = k * (1.0 + (a - 1.0) * ka_ref[...])
        lg = _cumsum_rows(lw, tril_b)
        e_neg = jnp.exp(-lg)
        prep.append(dict(a=-kk * jnp.exp(lg - lw), b=kk * a * e_neg, k=k2 * e_neg, r=r * jnp.exp(lg),
                         v=v, gam=jnp.exp(lg[CHUNK - 1:CHUNK, :]), gate=gate,
                         bonus=_half_sum(r * k2 * rk_ref[...], lo) * v))

    chunks = range(nc)
    bzero = jnp.zeros((), BF16)

    def bd(x):
        xb = x.astype(BF16)
        return jnp.concatenate([jnp.where(lo, xb, bzero), jnp.where(lo, bzero, xb)], axis=0)

    cast = lambda x: x.astype(BF16)
    dot = lambda a, b: jnp.dot(a, b, preferred_element_type=F32)
    dot_nt = lambda a, b: lax.dot_general(a, b, (((1,), (1,)), ((), ())), preferred_element_type=F32)
    dot_tn = lambda a, b: lax.dot_general(a, b, (((0,), (0,)), ((), ())), preferred_element_type=F32)
    get = lambda name: [prep[c][name] for c in chunks]
    at, bt, kt, rt, vv, gam = get("a"), get("b"), get("k"), get("r"), get("v"), get("gam")
    bd_b, bd_k, bd_v, bd_a = ([bd(x[c]) for c in chunks] for x in (bt, kt, vv, at))
    ar = [cast(jnp.concatenate([at[c], rt[c]], axis=0)) for c in chunks]
    pb = [dot_nt(ar[c], bd_b[c]) for c in chunks]
    pk = [dot_nt(ar[c], bd_k[c]) for c in chunks]
    n_ab = [jnp.where(stril, pb[c][:CHUNK], 0.0) for c in chunks]
    m_rb = [jnp.where(tril, pb[c][CHUNK:], 0.0) for c in chunks]
    n_ak = [jnp.where(stril, pk[c][:CHUNK], 0.0) for c in chunks]
    m_rk = [jnp.where(tril, pk[c][CHUNK:], 0.0) for c in chunks]
    t_inv = [eye_f + n_ab[c] for c in chunks]
    pw = n_ab
    bd_pw = [bd(pw[c]) for c in chunks]
    for _ in range(5):
        pw = [dot(cast(pw[c]), bd_pw[c]) for c in chunks]
        bd_pw = [bd(pw[c]) for c in chunks]
        t_inv = [t_inv[c] + dot(cast(t_inv[c]), bd_pw[c]) for c in chunks]
    akv = [dot(cast(n_ak[c]), bd_v[c]) for c in chunks]
    au = [dot(cast(t_inv[c]), jnp.concatenate([bd_a[c], bd(akv[c])], axis=1)) for c in chunks]
    a_hat = [au[c][:, 0:128] for c in chunks]
    u0 = [au[c][:, 128:256] for c in chunks]
    bd_ah = [bd(a_hat[c]) for c in chunks]
    bd_u0v = [jnp.concatenate([bd(u0[c]), bd_v[c]], axis=0) for c in chunks]
    g_lr = [dot_tn(bd_ah[c], bd_b[c]) * gam[c] for c in chunks]
    c0 = [dot_tn(bd_u0v[c], jnp.concatenate([bd_b[c], bd_k[c]], axis=0)) * gam[c] for c in chunks]
    r_hat = [rt[c] + dot(cast(m_rb[c]), bd_ah[c]) for c in chunks]
    y0 = [dot(cast(jnp.concatenate([m_rb[c], m_rk[c]], axis=1)), bd_u0v[c]) for c in chunks]

    s_cur = s_ref[...]
    for c in chunks:
        s_b = cast(s_cur)
        y = dot_nt(cast(r_hat[c]), s_b) + y0[c]
        s_cur = s_cur * gam[c] + dot(s_b, cast(g_lr[c])) + c0[c]
        mean = _half_sum(y, lo) * (1.0 / hd)
        yc = y - mean
        var = _half_sum(yc * yc, lo) * (1.0 / hd)
        y = yc * lax.rsqrt(var + RWKV_GN_EPS) * lng_ref[...] + lnb_ref[...]
        o_ref[c * CHUNK:(c + 1) * CHUNK, :] = ((y + prep[c]["bonus"]) * prep[c]["gate"]).astype(o_ref.dtype)
    s_ref[...] = s_cur

    sh_ref[7:8, :] = sh_ref[7 + lb:8 + lb, :]

    @pl.when(i == pl.num_programs(2) - 1)
    def _():
        sn_ref[0, 0] = s_ref[0:hd, 0:hd]
        sn_ref[0, 1] = s_ref[hd:2 * hd, hd:2 * hd]


def _rwkv_call(u, shift_buf, s0, p, b, l):
    lb = min(l, 1024)
    nb = l // lb
    row = lambda bb, h, i: bb * nb + i
    ublk = lambda off: pl.BlockSpec((lb, 128), lambda bb, h, i: (row(bb, h, i), off // 128 + h))
    sblk = lambda off: pl.BlockSpec((1, 1, 128), lambda bb, h, i: (bb, 0, off // 128 + h))
    vec = lambda: pl.BlockSpec((1, 128), lambda bb, h, i: (0, h))
    const = lambda shape: pl.BlockSpec(shape, lambda bb, h, i: (0,) * len(shape))
    return pl.pallas_call(
        functools.partial(_rwkv_kernel, nc=lb // CHUNK),
        name="rwkv_mixer",
        grid=(b, RWKV_HEADS // 2, nb),
        in_specs=[ublk(OFF_RWKV), ublk(OFF_RWKV + 1024), ublk(OFF_RWKV + 2048),
                  pl.BlockSpec((lb, 256), lambda bb, h, i: (row(bb, h, i), OFF_RLOW // 256)),
                  sblk(0), sblk(1024), sblk(2048),
                  pl.BlockSpec((1, 1, 256), lambda bb, h, i: (bb, 0, 3072 // 256)),
                  vec(), pl.BlockSpec((1, 128), lambda bb, h, i: (0, 8 + h)),
                  pl.BlockSpec((1, 128), lambda bb, h, i: (0, 16 + h)),
                  pl.BlockSpec((1, 256), lambda bb, h, i: (0, 3072 // 256)),
                  vec(), pl.BlockSpec((64, 128), lambda bb, h, i: (0, h)),
                  vec(), pl.BlockSpec((64, 128), lambda bb, h, i: (0, h)),
                  pl.BlockSpec((128, 128), lambda bb, h, i: (0, h)),
                  vec(), vec(), vec(), vec(), vec(),
                  pl.BlockSpec((1, 2, RWKV_HEAD, RWKV_HEAD), lambda bb, h, i: (bb, h, 0, 0))],
        out_specs=[pl.BlockSpec((lb, 128), lambda bb, h, i: (row(bb, h, i), h)),
                   pl.BlockSpec((1, 2, RWKV_HEAD, RWKV_HEAD), lambda bb, h, i: (bb, h, 0, 0))],
        out_shape=[jax.ShapeDtypeStruct((b * l, RWKV_DIM), BF16),
                   jax.ShapeDtypeStruct((b, RWKV_HEADS, RWKV_HEAD, RWKV_HEAD), F32)],
        scratch_shapes=[pltpu.VMEM((2 * RWKV_HEAD, 2 * RWKV_HEAD), F32),
                        pltpu.VMEM((lb + 8, 640), F32)],
        compiler_params=pltpu.CompilerParams(dimension_semantics=("parallel", "parallel", "arbitrary")),
    )(u, u, u, u, shift_buf, shift_buf, shift_buf, shift_buf,
      p["mu"], p["mu"], p["mu"], p["mu"], p["w0"], p["w2"], p["a0"], p["a2"], p["g2"],
      p["k_k"], p["k_a"], p["r_k"], p["ln_g"], p["ln_b"], s0)


def _ssd_kernel(z_ref, x_ref, bm_ref, cm_ref, dt_ref, cbx_ref, cbb_ref, cbc_ref,
                cwx_ref, cwb_ref, cwc_ref, cbiasx_ref, cbiasb_ref, cbiasc_ref,
                dtb_ref, alog_ref, dsk_ref, ng_ref, s0_ref,
                o_ref, sn_ref, s_ref, full_ref, *, nc):
    i = pl.program_id(2)
    lb = nc * CHUNK
    hpg = SSM_HEADS // SSM_GROUPS
    p = SSM_HEADDIM
    wx = hpg * p

    @pl.when(i == 0)
    def _():
        s_ref[...] = s0_ref[0].reshape(hpg * p, SSM_STATE)
        full_ref[5:8, 0:wx] = cbx_ref[0]
        full_ref[5:8, wx:wx + 128] = cbb_ref[0]
        full_ref[5:8, wx + 128:wx + 256] = cbc_ref[0]

    rnd = lambda a: a.astype(BF16).astype(F32)

    @pl.when(i == 0)
    def _():
        full_ref[5:8, :] = rnd(full_ref[5:8, :])
        full_ref[8 + lb:16 + lb, :] = jnp.zeros((8, wx + 256), F32)
        full_ref[0:5, :] = jnp.zeros((5, wx + 256), F32)

    full_ref[8:8 + lb, 0:wx] = rnd(x_ref[...])
    full_ref[8:8 + lb, wx:wx + 128] = rnd(bm_ref[...])
    full_ref[8:8 + lb, wx + 128:wx + 256] = rnd(cm_ref[...])

    tril = _tril_mask(CHUNK)
    tril_b = tril.astype(BF16)
    cw = jnp.concatenate([cwx_ref[...], cwb_ref[...], cwc_ref[...]], axis=1)
    cbias = jnp.concatenate([cbiasx_ref[...], cbiasb_ref[...], cbiasc_ref[...]], axis=1)
    a_row = -jnp.exp(alog_ref[0])

    iota = lambda shape, d: lax.broadcasted_iota(jnp.int32, shape, d)
    head_of = lambda idx: lax.shift_right_logical(idx, jnp.int32(6))
    expand = (iota((128, wx), 0) == head_of(iota((128, wx), 1))).astype(BF16)
    pos = jnp.bitwise_and(iota((CHUNK, wx), 1), jnp.int32(CHUNK - 1))
    tril_t = iota((CHUNK, wx), 0) >= pos
    diag_t = iota((CHUNK, wx), 0) == pos
    blockdiag = head_of(iota((wx, wx), 0)) == head_of(iota((wx, wx), 1))

    def split3(x):
        x1 = x.astype(BF16)
        r1 = x - x1.astype(F32)
        x2 = r1.astype(BF16)
        return x1, x2, (r1 - x2.astype(F32)).astype(BF16)

    dot = lambda a, b: jnp.dot(a, b, preferred_element_type=F32)
    per_head = lambda x: sum(dot(piece, expand) for piece in split3(x))

    chunks = range(nc)
    xs, bm, cm, dtv, acum = [], [], [], [], []
    win = CHUNK + 16
    shifts = [(iota((CHUNK, win), 1) == iota((CHUNK, win), 0) + jnp.int32(5 + j)).astype(BF16)
              for j in range(SSM_CONV - 1)]
    for c in chunks:
        base = c * CHUNK
        window = full_ref[base:base + win, :].astype(BF16)
        conv = cbias + cw[SSM_CONV - 1:SSM_CONV, :] * full_ref[8 + base:8 + base + CHUNK, :]
        for j in range(SSM_CONV - 1):
            conv = conv + cw[j:j + 1, :] * dot(shifts[j], window)
        conv = _silu(conv)
        xs.append(conv[:, 0:wx])
        bm.append(conv[:, wx:wx + 128])
        cm.append(conv[:, wx + 128:wx + 256])
        dtv.append(_softplus(dt_ref[base:base + CHUNK, :] + dtb_ref[0]))
    acum = [_cumsum_rows(dtv[c] * a_row, tril_b) for c in chunks]
    a_col = [per_head(acum[c]) for c in chunks]
    a_row_e = [jnp.sum(jnp.where(diag_t, a_col[c], 0.0), axis=0, keepdims=True)
               for c in chunks]
    dt_e = [per_head(dtv[c]) for c in chunks]
    cb_t = [_mm_nt(cm[c], jnp.concatenate([bm[c]] * hpg, axis=0)) for c in chunks]
    dec = [jnp.where(tril_t, jnp.exp(jnp.where(tril_t, a_col[c] - a_row_e[c], 0.0)), 0.0) for c in chunks]
    xdt = [xs[c] * dt_e[c] for c in chunks]
    xdt_bd = [jnp.where(blockdiag, jnp.concatenate([xdt[c].astype(BF16)] * hpg, axis=0),
                        jnp.zeros((), BF16)) for c in chunks]
    y_diag = [dot((cb_t[c] * dec[c]).astype(BF16), xdt_bd[c]) for c in chunks]
    a_last = [a_col[c][CHUNK - 1:CHUNK, :] for c in chunks]
    st = [_mm_tn(xdt[c] * jnp.exp(a_last[c] - a_col[c]), bm[c]) for c in chunks]
    e_end = [jnp.exp(acum[c][CHUNK - 1:CHUNK, :]) for c in chunks]
    e_in = [jnp.exp(a_col[c]) for c in chunks]
    s_cur = s_ref[...]
    for c in chunks:
        base = c * CHUNK
        y = y_diag[c] + _mm_nt(cm[c], s_cur) * e_in[c] + xs[c] * dsk_ref[...]
        s_cur = jnp.concatenate([s_cur[h * p:(h + 1) * p, :] * e_end[c][:, h:h + 1] for h in range(hpg)],
                                axis=0) + st[c]
        y = y * _silu(z_ref[base:base + CHUNK, :])
        y = y * lax.rsqrt(jnp.mean(y * y, -1, keepdims=True) + SSM_NORM_EPS) * ng_ref[...]
        o_ref[base:base + CHUNK, :] = y.astype(o_ref.dtype)
    s_ref[...] = s_cur

    full_ref[5:8, :] = full_ref[5 + lb:8 + lb, :]

    @pl.when(i == pl.num_programs(2) - 1)
    def _():
        sn_ref[0] = s_ref[...].reshape(hpg, p, SSM_STATE)


def _ssd_call(u, conv_buf, s0, p, b, l):
    lb = min(l, 256)
    nb = l // lb
    hpg = SSM_HEADS // SSM_GROUPS
    row = lambda bb, g, i: bb * nb + i
    ublk = lambda w, off: pl.BlockSpec((lb, w), lambda bb, g, i: (row(bb, g, i), off // w + g))
    gvec = lambda w, off: pl.BlockSpec((1, w), lambda bb, g, i: (0, off // w + g))
    gtap = lambda w, off: pl.BlockSpec((SSM_CONV, w), lambda bb, g, i: (0, off // w + g))
    gbuf = lambda w, off: pl.BlockSpec((1, SSM_CONV - 1, w), lambda bb, g, i: (bb, 0, off // w + g))
    return pl.pallas_call(
        functools.partial(_ssd_kernel, nc=lb // CHUNK),
        name="ssd_mixer",
        grid=(b, SSM_GROUPS, nb),
        in_specs=[ublk(512, OFF_SSM), ublk(512, OFF_SSM + 2048),
                  ublk(128, OFF_SSM + 4096), ublk(128, OFF_SSM + 4608), ublk(128, OFF_DT),
                  gbuf(512, 0), gbuf(128, 2048), gbuf(128, 2560),
                  gtap(512, 0), gtap(128, 2048), gtap(128, 2560),
                  gvec(512, 0), gvec(128, 2048), gvec(128, 2560),
                  pl.BlockSpec((1, 1, 128), lambda bb, g, i: (g, 0, 0)),
                  pl.BlockSpec((1, 1, 128), lambda bb, g, i: (g, 0, 0)),
                  gvec(512, 0), gvec(512, 0),
                  pl.BlockSpec((1, hpg, SSM_HEADDIM, SSM_STATE), lambda bb, g, i: (bb, g, 0, 0))],
        out_specs=[pl.BlockSpec((lb, 512), lambda bb, g, i: (row(bb, g, i), g)),
                   pl.BlockSpec((1, hpg, SSM_HEADDIM, SSM_STATE), lambda bb, g, i: (bb, g, 0, 0))],
        out_shape=[jax.ShapeDtypeStruct((b * l, SSM_INNER), BF16),
                   jax.ShapeDtypeStruct((b, SSM_HEADS, SSM_HEADDIM, SSM_STATE), F32)],
        scratch_shapes=[pltpu.VMEM((hpg * SSM_HEADDIM, SSM_STATE), F32),
                        pltpu.VMEM((lb + 16, 768), F32)],
        compiler_params=pltpu.CompilerParams(dimension_semantics=("parallel", "parallel", "arbitrary")),
    )(u, u, u, u, u, conv_buf, conv_buf, conv_buf,
      p["conv_w"], p["conv_w"], p["conv_w"], p["conv_b"], p["conv_b"], p["conv_b"],
      p["dt_bias"], p["a_log"], p["d_skip"], p["norm_g"], s0)


def _merge_kernel(gate_ref, og_ref, or_ref, os_ref, x_ref, wg_ref, wr_ref, ws_ref, wo_ref,
                  g_ref, b_ref, o_ref):
    d = D_MODEL
    m = _sigmoid(gate_ref[:, 0:d]) * jnp.dot(og_ref[...], wg_ref[...], preferred_element_type=F32)
    m = m + _sigmoid(gate_ref[:, d:2 * d]) * jnp.dot(or_ref[...], wr_ref[...], preferred_element_type=F32)
    m = m + _sigmoid(gate_ref[:, 2 * d:3 * d]) * jnp.dot(os_ref[...], ws_ref[...], preferred_element_type=F32)
    y = DN_ALPHA * x_ref[...] + _mm(m, wo_ref[...])
    o_ref[...] = _layer_norm(y, g_ref[...], b_ref[...])


def _merge_call(u, og, orr, os_, x, wg, wr, ws, wo, g, b):
    t = x.shape[0]
    tm = min(t, 256)
    tok = lambda w: pl.BlockSpec((tm, w), lambda i: (i, 0))
    const = lambda r, c: pl.BlockSpec((r, c), lambda i: (0, 0))
    return pl.pallas_call(
        _merge_kernel,
        name="merge_outproj_ln",
        grid=(t // tm,),
        in_specs=[tok(3 * D_MODEL), tok(D_MODEL), tok(D_MODEL), tok(SSM_INNER), tok(D_MODEL),
                  const(D_MODEL, D_MODEL), const(D_MODEL, D_MODEL), const(SSM_INNER, D_MODEL),
                  const(D_MODEL, D_MODEL), const(1, D_MODEL), const(1, D_MODEL)],
        out_specs=tok(D_MODEL),
        out_shape=jax.ShapeDtypeStruct((t, D_MODEL), F32),
        compiler_params=pltpu.CompilerParams(dimension_semantics=("parallel",),
                                             vmem_limit_bytes=VMEM_LIMIT),
    )(u, og, orr, os_, x, wg, wr, ws, wo, g, b)


def _router_gates(xb, wrt_ref, rb_ref, key_ref):
    tm = xb.shape[0]
    per_group = N_EXPERTS // N_EXPERT_GROUPS
    s_t = _sigmoid(_mm_nt(wrt_ref[...], xb))
    sb = s_t + rb_ref[...]
    gscore = []
    for g in range(N_EXPERT_GROUPS):
        xg = sb[g * per_group:(g + 1) * per_group, :]
        m1 = jnp.max(xg, axis=0, keepdims=True)
        eq = xg == m1
        cnt = jnp.sum(eq.astype(F32), axis=0, keepdims=True)
        m2 = jnp.max(jnp.where(eq, -jnp.inf, xg), axis=0, keepdims=True)
        gscore.append(m1 + jnp.where(cnt >= 2.0, m1, m2))
    for g in range(N_EXPERT_GROUPS):
        rank = jnp.zeros((1, tm), F32)
        for g2 in range(N_EXPERT_GROUPS):
            if g2 == g:
                continue
            beats = (gscore[g2] >= gscore[g]) if g2 < g else (gscore[g2] > gscore[g])
            rank = rank + beats.astype(F32)
        keep = rank < float(TOPK_GROUPS)
        key_ref[g * per_group:(g + 1) * per_group, :] = jnp.where(
            keep, sb[g * per_group:(g + 1) * per_group, :], -jnp.inf)
    key = key_ref[...]
    eidx = lax.broadcasted_iota(jnp.int32, (N_EXPERTS, tm), 0)

    def body(ep, rank):
        rowv = key_ref[pl.ds(ep, 1), :]
        beats = (rowv > key) | ((rowv == key) & (ep < eidx))
        return rank + beats.astype(F32)

    rank = lax.fori_loop(0, N_EXPERTS, body, jnp.zeros((N_EXPERTS, tm), F32))
    w = jnp.where(rank < float(TOP_K), s_t, 0.0)
    return w / jnp.sum(w, axis=0, keepdims=True) * ROUTED_SCALE


def _moe_kernel(x_ref, wrt_ref, rb_ref, w1_ref, w3_ref, w2_ref, s1_ref, s3_ref, s2_ref,
                g_ref, b_ref, o_ref, xb_ref, gates_ref, key_ref, *, sub):
    e = pl.program_id(1)
    tm = x_ref.shape[0]

    @pl.when(e == 0)
    def _():
        xb_ref[...] = x_ref[...].astype(BF16)
        g_t = _router_gates(xb_ref[...], wrt_ref, rb_ref, key_ref)
        gates_ref[...] = jnp.transpose(jnp.concatenate([g_t, jnp.zeros_like(g_t)], axis=0))
        for s in range(tm // sub):
            rows = pl.ds(s * sub, sub)
            xs = xb_ref[rows, :]
            hid = _silu(jnp.dot(xs, s1_ref[...], preferred_element_type=F32)) * jnp.dot(
                xs, s3_ref[...], preferred_element_type=F32)
            o_ref[rows, :] = _mm(hid, s2_ref[...])

    lane = lax.broadcasted_iota(jnp.int32, (1, 128), 1)
    per_step = w1_ref.shape[0]
    for s in range(tm // sub):
        rows = pl.ds(s * sub, sub)
        xs = xb_ref[rows, :]
        acc = o_ref[rows, :]
        for j in range(per_step):
            gcol = jnp.sum(jnp.where(lane == e * per_step + j, gates_ref[rows, :], 0.0),
                           axis=1, keepdims=True)
            hid = _silu(jnp.dot(xs, w1_ref[j], preferred_element_type=F32)) * jnp.dot(
                xs, w3_ref[j], preferred_element_type=F32)
            acc = acc + _mm(hid * gcol, w2_ref[j])
        o_ref[rows, :] = acc

    @pl.when(e == pl.num_programs(1) - 1)
    def _():
        o_ref[...] = _layer_norm(DN_ALPHA * x_ref[...] + o_ref[...], g_ref[...], b_ref[...])


def _moe_call(h, wrt, rb, w1, w3, w2, s1, s3, s2, g, b):
    t = h.shape[0]
    tm = min(t, 2048)
    sub = min(tm, 512)
    per_step = 2
    const = lambda r, c: pl.BlockSpec((r, c), lambda i, e: (0, 0))
    return pl.pallas_call(
        functools.partial(_moe_kernel, sub=sub),
        name="moe_ln",
        grid=(t // tm, N_EXPERTS // per_step),
        in_specs=[pl.BlockSpec((tm, D_MODEL), lambda i, e: (i, 0)),
                  const(N_EXPERTS, D_MODEL), const(N_EXPERTS, 1),
                  pl.BlockSpec((per_step, D_MODEL, D_EXPERT), lambda i, e: (e, 0, 0)),
                  pl.BlockSpec((per_step, D_MODEL, D_EXPERT), lambda i, e: (e, 0, 0)),
                  pl.BlockSpec((per_step, D_EXPERT, D_MODEL), lambda i, e: (e, 0, 0)),
                  const(D_MODEL, D_EXPERT), const(D_MODEL, D_EXPERT), const(D_EXPERT, D_MODEL),
                  const(1, D_MODEL), const(1, D_MODEL)],
        out_specs=pl.BlockSpec((tm, D_MODEL), lambda i, e: (i, 0)),
        out_shape=jax.ShapeDtypeStruct((t, D_MODEL), F32),
        scratch_shapes=[pltpu.VMEM((tm, D_MODEL), BF16),
                        pltpu.VMEM((tm, 128), F32),
                        pltpu.VMEM((N_EXPERTS, tm), F32)],
        compiler_params=pltpu.CompilerParams(dimension_semantics=("parallel", "arbitrary"),
                                             vmem_limit_bytes=VMEM_LIMIT),
    )(h, wrt, rb, w1, w3, w2, s1, s3, s2, g, b)


def _prep_layer(w_in, gla_wa2, gla_ba, gla_norm_g, gla_wo,
                rwkv_mu, rwkv_w0, rwkv_w2, rwkv_a0, rwkv_a2, rwkv_g2, rwkv_k_k, rwkv_k_a, rwkv_r_k,
                rwkv_ln_g, rwkv_ln_b, rwkv_wo,
                ssm_conv_w, ssm_conv_b, ssm_dt_bias, ssm_a_log, ssm_d, ssm_norm_g, ssm_wo,
                w_out, ln1_g, ln1_b,
                w_router, router_bias, exp_w1, exp_w3, exp_w2, sh_w1, sh_w3, sh_w2, ln2_g, ln2_b):
    d = D_MODEL
    o_gla, o_rwkv = 3 * d, 3 * d + 3088
    o_ssm = o_rwkv + RWKV_PROJ
    zeros = lambda n: jnp.zeros((d, n), w_in.dtype)
    dt0 = o_ssm + SSM_INNER + SSM_CONV_DIM
    hpg = SSM_HEADS // SSM_GROUPS
    dt_cols = []
    for g in range(SSM_GROUPS):
        dt_cols += [w_in[:, dt0 + g * hpg:dt0 + (g + 1) * hpg], zeros(128 - hpg)]
    w_u = jnp.concatenate(
        [w_in[:, 0:3 * d], w_in[:, o_gla:o_gla + 3072], w_in[:, o_rwkv:o_rwkv + 3072],
         w_in[:, o_ssm:o_ssm + SSM_INNER + SSM_CONV_DIM], w_in[:, o_rwkv + 3072:o_rwkv + 3328],
         w_in[:, o_gla + 3072:o_gla + 3088], zeros(128 - GLA_RANK)] + dt_cols + [zeros(128)],
        axis=1).astype(BF16)
    assert w_u.shape[1] == N_U
    row = lambda v: v.reshape(1, -1)
    pad_groups = lambda v: jnp.pad(v.reshape(SSM_GROUPS, 1, hpg), ((0, 0), (0, 0), (0, 128 - hpg)))
    return dict(
        w_u=w_u,
        gla=dict(wa2=jnp.pad(gla_wa2, ((0, 128 - GLA_RANK), (0, 0))).astype(BF16),
                 ba=row(gla_ba), ng=row(gla_norm_g)),
        rwkv=dict(mu=row(rwkv_mu), w0=row(rwkv_w0), w2=rwkv_w2.astype(BF16), a0=row(rwkv_a0),
                  a2=rwkv_a2.astype(BF16), g2=rwkv_g2.astype(BF16), k_k=row(rwkv_k_k),
                  k_a=row(rwkv_k_a), r_k=row(rwkv_r_k), ln_g=row(rwkv_ln_g), ln_b=row(rwkv_ln_b)),
        ssm=dict(conv_w=ssm_conv_w, conv_b=row(ssm_conv_b), dt_bias=pad_groups(ssm_dt_bias),
                 a_log=pad_groups(ssm_a_log), d_skip=row(jnp.repeat(ssm_d, SSM_HEADDIM)),
                 norm_g=row(ssm_norm_g)),
        merge=(gla_wo.astype(BF16), rwkv_wo.astype(BF16), ssm_wo.astype(BF16), w_out.astype(BF16),
               row(ln1_g), row(ln1_b)),
        moe=(jnp.transpose(w_router).astype(BF16), router_bias.reshape(-1, 1),
             exp_w1.astype(BF16), exp_w3.astype(BF16), exp_w2.astype(BF16),
             sh_w1.astype(BF16), sh_w3.astype(BF16), sh_w2.astype(BF16), row(ln2_g), row(ln2_b)),
    )


def _layer(h, st_gla, st_rwkv, st_shift, st_ssm, st_conv, lp, b, l):
    u = _inproj_call(h, lp["w_u"])
    o_gla, s_gla = _gla_call(u, st_gla, lp["gla"]["wa2"], lp["gla"]["ba"], lp["gla"]["ng"], b, l)
    o_rwkv, s_rwkv = _rwkv_call(u, st_shift, st_rwkv, lp["rwkv"], b, l)
    o_ssm, s_ssm = _ssd_call(u, st_conv, st_ssm, lp["ssm"], b, l)
    h1 = _merge_call(u, o_gla, o_rwkv, o_ssm, h, *lp["merge"])
    h2 = _moe_call(h1, *lp["moe"])
    u3 = u.reshape(b, l, N_U)
    shift_new = jnp.concatenate([u3[:, l - 1:, OFF_RWKV:OFF_RWKV + 3072],
                                 u3[:, l - 1:, OFF_RLOW:OFF_RLOW + 256]], axis=-1)
    conv_new = u3[:, l - (SSM_CONV - 1):, OFF_SSM + SSM_INNER:OFF_SSM + SSM_INNER + SSM_CONV_DIM]
    return h2, s_gla, s_rwkv, shift_new, s_ssm, conv_new


def _zero_states(batch):
    return (jnp.zeros((batch, GLA_HEADS, GLA_DK, GLA_DV), F32),
            jnp.zeros((batch, RWKV_HEADS, RWKV_HEAD, RWKV_HEAD), F32),
            jnp.zeros((batch, 1, RWKV_PROJ), F32),
            jnp.zeros((batch, SSM_HEADS, SSM_HEADDIM, SSM_STATE), F32),
            jnp.zeros((batch, SSM_CONV - 1, SSM_CONV_DIM), F32))


def kernel(x_prompt, x_sample, state_gla, state_rwkv, state_rwkv_shift, state_ssm, state_ssm_conv,
           ln_in_g, ln_in_b, w_in, gla_wa2, gla_ba, gla_norm_g, gla_wo,
           rwkv_mu, rwkv_w0, rwkv_w2, rwkv_a0, rwkv_a2, rwkv_g2, rwkv_k_k, rwkv_k_a, rwkv_r_k,
           rwkv_ln_g, rwkv_ln_b, rwkv_wo,
           ssm_conv_w, ssm_conv_b, ssm_dt_bias, ssm_a_log, ssm_d, ssm_norm_g, ssm_wo,
           w_out, ln1_g, ln1_b,
           w_router, router_bias, exp_w1, exp_w3, exp_w2, sh_w1, sh_w3, sh_w2, ln2_g, ln2_b):
    layer_params = (w_in, gla_wa2, gla_ba, gla_norm_g, gla_wo,
                    rwkv_mu, rwkv_w0, rwkv_w2, rwkv_a0, rwkv_a2, rwkv_g2, rwkv_k_k, rwkv_k_a, rwkv_r_k,
                    rwkv_ln_g, rwkv_ln_b, rwkv_wo,
                    ssm_conv_w, ssm_conv_b, ssm_dt_bias, ssm_a_log, ssm_d, ssm_norm_g, ssm_wo,
                    w_out, ln1_g, ln1_b,
                    w_router, router_bias, exp_w1, exp_w3, exp_w2, sh_w1, sh_w3, sh_w2, ln2_g, ln2_b)
    depth = w_in.shape[0]
    bp, lp_, d = x_prompt.shape
    bs, ls, _ = x_sample.shape
    h_p = _ln_call(x_prompt.reshape(bp * lp_, d), ln_in_g, ln_in_b)
    h_s = _ln_call(x_sample.reshape(bs * ls, d), ln_in_g, ln_in_b)
    new_p, new_s = [], []
    for i in range(depth):
        lp = _prep_layer(*[t[i] for t in layer_params])
        h_p, *st_p = _layer(h_p, *_zero_states(bp), lp, bp, lp_)
        new_p.append(st_p)
        h_s, *st_s = _layer(h_s, state_gla[i], state_rwkv[i], state_rwkv_shift[i], state_ssm[i],
                            state_ssm_conv[i], lp, bs, ls)
        new_s.append(st_s)
    stack = lambda rows: tuple(jnp.stack(z) for z in zip(*rows))
    return (h_p.reshape(bp, lp_, d), h_s.reshape(bs, ls, d)) + stack(new_p) + stack(new_s)
```

```python
import functools

import jax
import jax.numpy as jnp
from jax import lax
from jax.experimental import pallas as pl
from jax.experimental.pallas import tpu as pltpu

F32 = jnp.float32
BF16 = jnp.bfloat16

D_MODEL = 1024
CHUNK = 64
GLA_HEADS, GLA_DK, GLA_DV, GLA_RANK = 4, 128, 256, 16
GLA_GATE_NORM = 16.0
GLA_NORM_EPS = 1e-5
RWKV_HEAD, RWKV_HEADS, RWKV_DIM = 64, 16, 1024
RWKV_PROJ = 3 * RWKV_DIM + 64 + 64 + 128
RWKV_GN_EPS = 64e-5
SSM_INNER, SSM_HEADDIM, SSM_HEADS, SSM_GROUPS, SSM_STATE, SSM_CONV = 2048, 64, 32, 4, 128, 4
SSM_CONV_DIM = SSM_INNER + 2 * SSM_GROUPS * SSM_STATE
SSM_NORM_EPS = 1e-5
N_EXPERTS, TOP_K, N_EXPERT_GROUPS, TOPK_GROUPS, D_EXPERT = 64, 8, 8, 4, 256
ROUTED_SCALE = 2.5
DEPTH = 2
DN_ALPHA = (2.0 * DEPTH) ** 0.25
LN_EPS = 1e-5

OFF_GATE, OFF_GLA, OFF_RWKV, OFF_SSM = 0, 3072, 6144, 9216
OFF_RLOW, OFF_GALOW, OFF_DT = 14336, 14592, 14720
N_U = 15360
VMEM_LIMIT = 56 * 1024 * 1024


def _mm(a, b):
    return jnp.dot(a.astype(BF16), b.astype(BF16), preferred_element_type=F32)


def _mm_nt(a, b):
    return lax.dot_general(a.astype(BF16), b.astype(BF16), (((1,), (1,)), ((), ())),
                           preferred_element_type=F32)


def _mm_tn(a, b):
    return lax.dot_general(a.astype(BF16), b.astype(BF16), (((0,), (0,)), ((), ())),
                           preferred_element_type=F32)


def _sigmoid(x):
    return 1.0 / (1.0 + jnp.exp(-x))


def _silu(x):
    return x * _sigmoid(x)


def _softplus(x):
    return jnp.maximum(x, 0.0) + jnp.log(1.0 + jnp.exp(-jnp.abs(x)))


def _layer_norm(x, g, b):
    mu = jnp.mean(x, -1, keepdims=True)
    xc = x - mu
    var = jnp.mean(xc * xc, -1, keepdims=True)
    return xc * lax.rsqrt(var + LN_EPS) * g + b


def _tril_mask(n, strict=False):
    r = lax.broadcasted_iota(jnp.int32, (n, n), 0)
    c = lax.broadcasted_iota(jnp.int32, (n, n), 1)
    return (r > c) if strict else (r >= c)


def _cumsum_rows(x, tril_bf16):
    x1 = x.astype(BF16)
    r1 = x - x1.astype(F32)
    x2 = r1.astype(BF16)
    x3 = (r1 - x2.astype(F32)).astype(BF16)
    dot = lambda p: jnp.dot(tril_bf16, p, preferred_element_type=F32)
    return dot(x1) + dot(x2) + dot(x3)


def _ln_kernel(x_ref, g_ref, b_ref, o_ref):
    o_ref[...] = _layer_norm(x_ref[...], g_ref[...], b_ref[...])


def _ln_call(x, g, b):
    t = x.shape[0]
    tm = min(t, 512)
    return pl.pallas_call(
        _ln_kernel,
        name="ln_in",
        grid=(t // tm,),
        in_specs=[pl.BlockSpec((tm, D_MODEL), lambda i: (i, 0)),
                  pl.BlockSpec((1, D_MODEL), lambda i: (0, 0)),
                  pl.BlockSpec((1, D_MODEL), lambda i: (0, 0))],
        out_specs=pl.BlockSpec((tm, D_MODEL), lambda i: (i, 0)),
        out_shape=jax.ShapeDtypeStruct((t, D_MODEL), F32),
        compiler_params=pltpu.CompilerParams(dimension_semantics=("parallel",)),
    )(x, g.reshape(1, -1), b.reshape(1, -1))


def _inproj_kernel(x_ref, w_ref, o_ref, xb_ref):
    @pl.when(pl.program_id(1) == 0)
    def _():
        xb_ref[...] = x_ref[...].astype(BF16)

    o_ref[...] = jnp.dot(xb_ref[...], w_ref[...], preferred_element_type=F32)


def _inproj_call(h, w):
    t, n = h.shape[0], w.shape[1]
    tm = min(t, 2048)
    tn = 1024
    return pl.pallas_call(
        _inproj_kernel,
        name="inproj",
        grid=(t // tm, n // tn),
        in_specs=[pl.BlockSpec((tm, D_MODEL), lambda i, j: (i, 0)),
                  pl.BlockSpec((D_MODEL, tn), lambda i, j: (0, j))],
        out_specs=pl.BlockSpec((tm, tn), lambda i, j: (i, j)),
        out_shape=jax.ShapeDtypeStruct((t, n), F32),
        scratch_shapes=[pltpu.VMEM((tm, D_MODEL), BF16)],
        compiler_params=pltpu.CompilerParams(dimension_semantics=("parallel", "arbitrary"),
                                             vmem_limit_bytes=VMEM_LIMIT),
    )(h, w)


def _gla_kernel(q_ref, k_ref, v_ref, r_ref, al_ref, wa2_ref, ba_ref, ng_ref, s0_ref,
                o_ref, sn_ref, s_ref, *, nc):
    i = pl.program_id(2)

    @pl.when(i == 0)
    def _():
        s_ref[...] = s0_ref[0, 0]

    tril = _tril_mask(CHUNK)
    tril_b = tril.astype(BF16)
    eye = (lax.broadcasted_iota(jnp.int32, (GLA_DK, GLA_DK), 0)
           == lax.broadcasted_iota(jnp.int32, (GLA_DK, GLA_DK), 1))
    chunks = range(nc)
    sl = [pl.ds(c * CHUNK, CHUNK) for c in chunks]
    z = _mm(al_ref[...], wa2_ref[...]) + ba_ref[...]
    la = -_softplus(-z) * (1.0 / GLA_GATE_NORM)
    g = [_cumsum_rows(la[c * CHUNK:(c + 1) * CHUNK], tril_b) for c in chunks]
    g_last = [g[c][CHUNK - 1:CHUNK, :] for c in chunks]
    q_dec = [q_ref[sl[c], :] * jnp.exp(g[c]) * (GLA_DK ** -0.5) for c in chunks]
    k_inv = [k_ref[sl[c], :] * jnp.exp(-g[c]) for c in chunks]
    k_end = [k_ref[sl[c], :] * jnp.exp(g_last[c] - g[c]) for c in chunks]
    scores = [jnp.where(tril, _mm_nt(q_dec[c], k_inv[c]), 0.0) for c in chunks]
    kv = [_mm_tn(k_end[c], v_ref[sl[c], :]) for c in chunks]
    o_intra = [_mm(scores[c], v_ref[sl[c], :]) for c in chunks]
    dec_col = [jnp.sum(jnp.where(eye, jnp.exp(g_last[c]), 0.0), axis=1, keepdims=True) for c in chunks]
    s_cur = s_ref[...]
    for c in chunks:
        o = o_intra[c] + _mm(q_dec[c], s_cur)
        s_cur = s_cur * dec_col[c] + kv[c]
        o = o * lax.rsqrt(jnp.mean(o * o, -1, keepdims=True) + GLA_NORM_EPS) * ng_ref[...]
        o_ref[sl[c], :] = (o * _silu(r_ref[sl[c], :])).astype(o_ref.dtype)
    s_ref[...] = s_cur

    @pl.when(i == pl.num_programs(2) - 1)
    def _():
        sn_ref[0, 0] = s_ref[...]


def _gla_call(u, s0, wa2p, ba, ng, b, l):
    lb = min(l, 512)
    nb = l // lb
    row = lambda bb, h, i: bb * nb + i
    spec = lambda w, off: pl.BlockSpec((lb, w), lambda bb, h, i: (row(bb, h, i), off + h))
    return pl.pallas_call(
        functools.partial(_gla_kernel, nc=lb // CHUNK),
        name="gla_mixer",
        grid=(b, GLA_HEADS, nb),
        in_specs=[spec(128, OFF_GLA // 128), spec(128, (OFF_GLA + 512) // 128),
                  spec(256, (OFF_GLA + 1024) // 256), spec(256, (OFF_GLA + 2048) // 256),
                  pl.BlockSpec((lb, 128), lambda bb, h, i: (row(bb, h, i), OFF_GALOW // 128)),
                  pl.BlockSpec((128, 128), lambda bb, h, i: (0, h)),
                  pl.BlockSpec((1, 128), lambda bb, h, i: (0, h)),
                  pl.BlockSpec((1, 256), lambda bb, h, i: (0, 0)),
                  pl.BlockSpec((1, 1, GLA_DK, GLA_DV), lambda bb, h, i: (bb, h, 0, 0))],
        out_specs=[pl.BlockSpec((lb, 256), lambda bb, h, i: (row(bb, h, i), h)),
                   pl.BlockSpec((1, 1, GLA_DK, GLA_DV), lambda bb, h, i: (bb, h, 0, 0))],
        out_shape=[jax.ShapeDtypeStruct((b * l, GLA_HEADS * GLA_DV), BF16),
                   jax.ShapeDtypeStruct((b, GLA_HEADS, GLA_DK, GLA_DV), F32)],
        scratch_shapes=[pltpu.VMEM((GLA_DK, GLA_DV), F32)],
        compiler_params=pltpu.CompilerParams(dimension_semantics=("parallel", "parallel", "arbitrary")),
    )(u, u, u, u, u, wa2p, ba, ng, s0)


def _half_sum(x, lo):
    s0 = jnp.sum(jnp.where(lo, x, 0.0), -1, keepdims=True)
    s1 = jnp.sum(jnp.where(lo, 0.0, x), -1, keepdims=True)
    return jnp.where(lo, s0, s1)


def _rwkv_kernel(r_ref, k_ref, v_ref, low_ref, shr_ref, shk_ref, shv_ref, shl_ref,
                 mur_ref, muk_ref, muv_ref, mul_ref, w0_ref, w2_ref, a0_ref, a2_ref, g2_ref,
                 kk_ref, ka_ref, rk_ref, lng_ref, lnb_ref, s0_ref,
                 o_ref, sn_ref, s_ref, sh_ref, *, nc):
    i = pl.program_id(2)
    lb = nc * CHUNK
    hd = RWKV_HEAD

    @pl.when(i == 0)
    def _():
        zero = jnp.zeros((hd, hd), F32)
        s_ref[...] = jnp.concatenate([jnp.concatenate([s0_ref[0, 0], zero], axis=1),
                                      jnp.concatenate([zero, s0_ref[0, 1]], axis=1)], axis=0)
        sh_ref[7:8, 0:128] = shr_ref[0]
        sh_ref[7:8, 128:256] = shk_ref[0]
        sh_ref[7:8, 256:384] = shv_ref[0]
        sh_ref[7:8, 384:640] = shl_ref[0]

    sh_ref[8:8 + lb, 0:128] = r_ref[...]
    sh_ref[8:8 + lb, 128:256] = k_ref[...]
    sh_ref[8:8 + lb, 256:384] = v_ref[...]
    sh_ref[8:8 + lb, 384:640] = low_ref[...]

    tril_b = _tril_mask(CHUNK).astype(BF16)
    row = lax.broadcasted_iota(jnp.int32, (CHUNK, 128), 0)
    pos = jnp.bitwise_and(lax.broadcasted_iota(jnp.int32, (CHUNK, 128), 1), jnp.int32(hd - 1))
    tril, stril, eye_f = row >= pos, row > pos, (row == pos).astype(F32)
    lo = lax.broadcasted_iota(jnp.int32, (1, 128), 1) < hd

    prep = []
    for c in range(nc):
        cur = sh_ref[8 + c * CHUNK:8 + (c + 1) * CHUNK, :]
        prev = sh_ref[7 + c * CHUNK:7 + (c + 1) * CHUNK, :]
        shift = lambda a, b, mu: a + (b - a) * mu
        r = shift(cur[:, 0:128], prev[:, 0:128], mur_ref[...])
        k = shift(cur[:, 128:256], prev[:, 128:256], muk_ref[...])
        v = shift(cur[:, 256:384], prev[:, 256:384], muv_ref[...])
        low = shift(cur[:, 384:640], prev[:, 384:640], mul_ref[...])
        zw = w0_ref[...] + _mm(jnp.tanh(low[:, 0:64]), w2_ref[...])
        lw = -jnp.exp(-_softplus(-zw) - 0.5)
        a = _sigmoid(a0_ref[...] + _mm(low[:, 64:128], a2_ref[...]))
        gate = _mm(_sigmoid(low[:, 128:256]), g2_ref[...])
        kk = k * kk_ref[...]
        kk = kk * lax.rsqrt(jnp.maximum(_half_sum(kk * kk, lo), 1e-24))
        k2 = k * (1.0 + (a - 1.0) * ka_ref[...])
        lg = _cumsum_rows(lw, tril_b)
        e_neg = jnp.exp(-lg)
        prep.append(dict(a=-kk * jnp.exp(lg - lw), b=kk * a * e_neg, k=k2 * e_neg, r=r * jnp.exp(lg),
                         v=v, gam=jnp.exp(lg[CHUNK - 1:CHUNK, :]), gate=gate,
                         bonus=_half_sum(r * k2 * rk_ref[...], lo) * v))

    chunks = range(nc)
    bzero = jnp.zeros((), BF16)

    def bd(x):
        xb = x.astype(BF16)
        return jnp.concatenate([jnp.where(lo, xb, bzero), jnp.where(lo, bzero, xb)], axis=0)

    cast = lambda x: x.astype(BF16)
    dot = lambda a, b: jnp.dot(a, b, preferred_element_type=F32)
    dot_nt = lambda a, b: lax.dot_general(a, b, (((1,), (1,)), ((), ())), preferred_element_type=F32)
    dot_tn = lambda a, b: lax.dot_general(a, b, (((0,), (0,)), ((), ())), preferred_element_type=F32)
    get = lambda name: [prep[c][name] for c in chunks]
    at, bt, kt, rt, vv, gam = get("a"), get("b"), get("k"), get("r"), get("v"), get("gam")
    bd_b, bd_k, bd_v, bd_a = ([bd(x[c]) for c in chunks] for x in (bt, kt, vv, at))
    ar = [cast(jnp.concatenate([at[c], rt[c]], axis=0)) for c in chunks]
    pb = [dot_nt(ar[c], bd_b[c]) for c in chunks]
    pk = [dot_nt(ar[c], bd_k[c]) for c in chunks]
    n_ab = [jnp.where(stril, pb[c][:CHUNK], 0.0) for c in chunks]
    m_rb = [jnp.where(tril, pb[c][CHUNK:], 0.0) for c in chunks]
    n_ak = [jnp.where(stril, pk[c][:CHUNK], 0.0) for c in chunks]
    m_rk = [jnp.where(tril, pk[c][CHUNK:], 0.0) for c in chunks]
    t_inv = [eye_f + n_ab[c] for c in chunks]
    pw = n_ab
    bd_pw = [bd(pw[c]) for c in chunks]
    for _ in range(5):
        pw = [dot(cast(pw[c]), bd_pw[c]) for c in chunks]
        bd_pw = [bd(pw[c]) for c in chunks]
        t_inv = [t_inv[c] + dot(cast(t_inv[c]), bd_pw[c]) for c in chunks]
    akv = [dot(cast(n_ak[c]), bd_v[c]) for c in chunks]
    au = [dot(cast(t_inv[c]), jnp.concatenate([bd_a[c], bd(akv[c])], axis=1)) for c in chunks]
    a_hat = [au[c][:, 0:128] for c in chunks]
    u0 = [au[c][:, 128:256] for c in chunks]
    bd_ah = [bd(a_hat[c]) for c in chunks]
    bd_u0v = [jnp.concatenate([bd(u0[c]), bd_v[c]], axis=0) for c in chunks]
    g_lr = [dot_tn(bd_ah[c], bd_b[c]) * gam[c] for c in chunks]
    c0 = [dot_tn(bd_u0v[c], jnp.concatenate([bd_b[c], bd_k[c]], axis=0)) * gam[c] for c in chunks]
    r_hat = [rt[c] + dot(cast(m_rb[c]), bd_ah[c]) for c in chunks]
    y0 = [dot(cast(jnp.concatenate([m_rb[c], m_rk[c]], axis=1)), bd_u0v[c]) for c in chunks]

    s_cur = s_ref[...]
    for c in chunks:
        s_b = cast(s_cur)
        y = dot_nt(cast(r_hat[c]), s_b) + y0[c]
        s_cur = s_cur * gam[c] + dot(s_b, cast(g_lr[c])) + c0[c]
        mean = _half_sum(y, lo) * (1.0 / hd)
        yc = y - mean
        var = _half_sum(yc * yc, lo) * (1.0 / hd)
        y = yc * lax.rsqrt(var + RWKV_GN_EPS) * lng_ref[...] + lnb_ref[...]
        o_ref[c * CHUNK:(c + 1) * CHUNK, :] = ((y + prep[c]["bonus"]) * prep[c]["gate"]).astype(o_ref.dtype)
    s_ref[...] = s_cur

    sh_ref[7:8, :] = sh_ref[7 + lb:8 + lb, :]

    @pl.when(i == pl.num_programs(2) - 1)
    def _():
        sn_ref[0, 0] = s_ref[0:hd, 0:hd]
        sn_ref[0, 1] = s_ref[hd:2 * hd, hd:2 * hd]


def _rwkv_call(u, shift_buf, s0, p, b, l):
    lb = min(l, 1024)
    nb = l // lb
    row = lambda bb, h, i: bb * nb + i
    ublk = lambda off: pl.BlockSpec((lb, 128), lambda bb, h, i: (row(bb, h, i), off // 128 + h))
    sblk = lambda off: pl.BlockSpec((1, 1, 128), lambda bb, h, i: (bb, 0, off // 128 + h))
    vec = lambda: pl.BlockSpec((1, 128), lambda bb, h, i: (0, h))
    const = lambda shape: pl.BlockSpec(shape, lambda bb, h, i: (0,) * len(shape))
    return pl.pallas_call(
        functools.partial(_rwkv_kernel, nc=lb // CHUNK),
        name="rwkv_mixer",
        grid=(b, RWKV_HEADS // 2, nb),
        in_specs=[ublk(OFF_RWKV), ublk(OFF_RWKV + 1024), ublk(OFF_RWKV + 2048),
                  pl.BlockSpec((lb, 256), lambda bb, h, i: (row(bb, h, i), OFF_RLOW // 256)),
                  sblk(0), sblk(1024), sblk(2048),
                  pl.BlockSpec((1, 1, 256), lambda bb, h, i: (bb, 0, 3072 // 256)),
                  vec(), pl.BlockSpec((1, 128), lambda bb, h, i: (0, 8 + h)),
                  pl.BlockSpec((1, 128), lambda bb, h, i: (0, 16 + h)),
                  pl.BlockSpec((1, 256), lambda bb, h, i: (0, 3072 // 256)),
                  vec(), pl.BlockSpec((64, 128), lambda bb, h, i: (0, h)),
                  vec(), pl.BlockSpec((64, 128), lambda bb, h, i: (0, h)),
                  pl.BlockSpec((128, 128), lambda bb, h, i: (0, h)),
                  vec(), vec(), vec(), vec(), vec(),
                  pl.BlockSpec((1, 2, RWKV_HEAD, RWKV_HEAD), lambda bb, h, i: (bb, h, 0, 0))],
        out_specs=[pl.BlockSpec((lb, 128), lambda bb, h, i: (row(bb, h, i), h)),
                   pl.BlockSpec((1, 2, RWKV_HEAD, RWKV_HEAD), lambda bb, h, i: (bb, h, 0, 0))],
        out_shape=[jax.ShapeDtypeStruct((b * l, RWKV_DIM), BF16),
                   jax.ShapeDtypeStruct((b, RWKV_HEADS, RWKV_HEAD, RWKV_HEAD), F32)],
        scratch_shapes=[pltpu.VMEM((2 * RWKV_HEAD, 2 * RWKV_HEAD), F32),
                        pltpu.VMEM((lb + 8, 640), F32)],
        compiler_params=pltpu.CompilerParams(dimension_semantics=("parallel", "parallel", "arbitrary")),
    )(u, u, u, u, shift_buf, shift_buf, shift_buf, shift_buf,
      p["mu"], p["mu"], p["mu"], p["mu"], p["w0"], p["w2"], p["a0"], p["a2"], p["g2"],
      p["k_k"], p["k_a"], p["r_k"], p["ln_g"], p["ln_b"], s0)


def _ssd_kernel(z_ref, x_ref, bm_ref, cm_ref, dt_ref, cbx_ref, cbb_ref, cbc_ref,
                cwx_ref, cwb_ref, cwc_ref, cbiasx_ref, cbiasb_ref, cbiasc_ref,
                dtb_ref, alog_ref, dsk_ref, ng_ref, s0_ref,
                o_ref, sn_ref, s_ref, full_ref, *, nc):
    i = pl.program_id(2)
    lb = nc * CHUNK
    hpg = SSM_HEADS // SSM_GROUPS
    p = SSM_HEADDIM
    wx = hpg * p

    @pl.when(i == 0)
    def _():
        s_ref[...] = s0_ref[0].reshape(hpg * p, SSM_STATE)
        full_ref[5:8, 0:wx] = cbx_ref[0]
        full_ref[5:8, wx:wx + 128] = cbb_ref[0]
        full_ref[5:8, wx + 128:wx + 256] = cbc_ref[0]

    rnd = lambda a: a.astype(BF16).astype(F32)

    @pl.when(i == 0)
    def _():
        full_ref[5:8, :] = rnd(full_ref[5:8, :])
        full_ref[8 + lb:16 + lb, :] = jnp.zeros((8, wx + 256), F32)
        full_ref[0:5, :] = jnp.zeros((5, wx + 256), F32)

    full_ref[8:8 + lb, 0:wx] = rnd(x_ref[...])
    full_ref[8:8 + lb, wx:wx + 128] = rnd(bm_ref[...])
    full_ref[8:8 + lb, wx + 128:wx + 256] = rnd(cm_ref[...])

    tril = _tril_mask(CHUNK)
    tril_b = tril.astype(BF16)
    cw = jnp.concatenate([cwx_ref[...], cwb_ref[...], cwc_ref[...]], axis=1)
    cbias = jnp.concatenate([cbiasx_ref[...], cbiasb_ref[...], cbiasc_ref[...]], axis=1)
    a_row = -jnp.exp(alog_ref[0])

    iota = lambda shape, d: lax.broadcasted_iota(jnp.int32, shape, d)
    head_of = lambda idx: lax.shift_right_logical(idx, jnp.int32(6))
    expand = (iota((128, wx), 0) == head_of(iota((128, wx), 1))).astype(BF16)
    pos = jnp.bitwise_and(iota((CHUNK, wx), 1), jnp.int32(CHUNK - 1))
    tril_t = iota((CHUNK, wx), 0) >= pos
    diag_t = iota((CHUNK, wx), 0) == pos
    blockdiag = head_of(iota((wx, wx), 0)) == head_of(iota((wx, wx), 1))

    def split3(x):
        x1 = x.astype(BF16)
        r1 = x - x1.astype(F32)
        x2 = r1.astype(BF16)
        return x1, x2, (r1 - x2.astype(F32)).astype(BF16)

    dot = lambda a, b: jnp.dot(a, b, preferred_element_type=F32)
    per_head = lambda x: sum(dot(piece, expand) for piece in split3(x))

    chunks = range(nc)
    xs, bm, cm, dtv, acum = [], [], [], [], []
    win = CHUNK + 16
    shifts = [(iota((CHUNK, win), 1) == iota((CHUNK, win), 0) + jnp.int32(5 + j)).astype(BF16)
              for j in range(SSM_CONV - 1)]
    for c in chunks:
        base = c * CHUNK
        window = full_ref[base:base + win, :].astype(BF16)
        conv = cbias + cw[SSM_CONV - 1:SSM_CONV, :] * full_ref[8 + base:8 + base + CHUNK, :]
        for j in range(SSM_CONV - 1):
            conv = conv + cw[j:j + 1, :] * dot(shifts[j], window)
        conv = _silu(conv)
        xs.append(conv[:, 0:wx])
        bm.append(conv[:, wx:wx + 128])
        cm.append(conv[:, wx + 128:wx + 256])
        dtv.append(_softplus(dt_ref[base:base + CHUNK, :] + dtb_ref[0]))
    acum = [_cumsum_rows(dtv[c] * a_row, tril_b) for c in chunks]
    a_col = [per_head(acum[c]) for c in chunks]
    a_row_e = [jnp.sum(jnp.where(diag_t, a_col[c], 0.0), axis=0, keepdims=True)
               for c in chunks]
    dt_e = [per_head(dtv[c]) for c in chunks]
    cb_t = [_mm_nt(cm[c], jnp.concatenate([bm[c]] * hpg, axis=0)) for c in chunks]
    dec = [jnp.where(tril_t, jnp.exp(jnp.where(tril_t, a_col[c] - a_row_e[c], 0.0)), 0.0) for c in chunks]
    xdt = [xs[c] * dt_e[c] for c in chunks]
    xdt_bd = [jnp.where(blockdiag, jnp.concatenate([xdt[c].astype(BF16)] * hpg, axis=0),
                        jnp.zeros((), BF16)) for c in chunks]
    y_diag = [dot((cb_t[c] * dec[c]).astype(BF16), xdt_bd[c]) for c in chunks]
    a_last = [a_col[c][CHUNK - 1:CHUNK, :] for c in chunks]
    st = [_mm_tn(xdt[c] * jnp.exp(a_last[c] - a_col[c]), bm[c]) for c in chunks]
    e_end = [jnp.exp(acum[c][CHUNK - 1:CHUNK, :]) for c in chunks]
    e_in = [jnp.exp(a_col[c]) for c in chunks]
    s_cur = s_ref[...]
    for c in chunks:
        base = c * CHUNK
        y = y_diag[c] + _mm_nt(cm[c], s_cur) * e_in[c] + xs[c] * dsk_ref[...]
        s_cur = jnp.concatenate([s_cur[h * p:(h + 1) * p, :] * e_end[c][:, h:h + 1] for h in range(hpg)],
                                axis=0) + st[c]
        y = y * _silu(z_ref[base:base + CHUNK, :])
        y = y * lax.rsqrt(jnp.mean(y * y, -1, keepdims=True) + SSM_NORM_EPS) * ng_ref[...]
        o_ref[base:base + CHUNK, :] = y.astype(o_ref.dtype)
    s_ref[...] = s_cur

    full_ref[5:8, :] = full_ref[5 + lb:8 + lb, :]

    @pl.when(i == pl.num_programs(2) - 1)
    def _():
        sn_ref[0] = s_ref[...].reshape(hpg, p, SSM_STATE)


def _ssd_call(u, conv_buf, s0, p, b, l):
    lb = min(l, 256)
    nb = l // lb
    hpg = SSM_HEADS // SSM_GROUPS
    row = lambda bb, g, i: bb * nb + i
    ublk = lambda w, off: pl.BlockSpec((lb, w), lambda bb, g, i: (row(bb, g, i), off // w + g))
    gvec = lambda w, off: pl.BlockSpec((1, w), lambda bb, g, i: (0, off // w + g))
    gtap = lambda w, off: pl.BlockSpec((SSM_CONV, w), lambda bb, g, i: (0, off // w + g))
    gbuf = lambda w, off: pl.BlockSpec((1, SSM_CONV - 1, w), lambda bb, g, i: (bb, 0, off // w + g))
    return pl.pallas_call(
        functools.partial(_ssd_kernel, nc=lb // CHUNK),
        name="ssd_mixer",
        grid=(b, SSM_GROUPS, nb),
        in_specs=[ublk(512, OFF_SSM), ublk(512, OFF_SSM + 2048),
                  ublk(128, OFF_SSM + 4096), ublk(128, OFF_SSM + 4608), ublk(128, OFF_DT),
                  gbuf(512, 0), gbuf(128, 2048), gbuf(128, 2560),
                  gtap(512, 0), gtap(128, 2048), gtap(128, 2560),
                  gvec(512, 0), gvec(128, 2048), gvec(128, 2560),
                  pl.BlockSpec((1, 1, 128), lambda bb, g, i: (g, 0, 0)),
                  pl.BlockSpec((1, 1, 128), lambda bb, g, i: (g, 0, 0)),
                  gvec(512, 0), gvec(512, 0),
                  pl.BlockSpec((1, hpg, SSM_HEADDIM, SSM_STATE), lambda bb, g, i: (bb, g, 0, 0))],
        out_specs=[pl.BlockSpec((lb, 512), lambda bb, g, i: (row(bb, g, i), g)),
                   pl.BlockSpec((1, hpg, SSM_HEADDIM, SSM_STATE), lambda bb, g, i: (bb, g, 0, 0))],
        out_shape=[jax.ShapeDtypeStruct((b * l, SSM_INNER), BF16),
                   jax.ShapeDtypeStruct((b, SSM_HEADS, SSM_HEADDIM, SSM_STATE), F32)],
        scratch_shapes=[pltpu.VMEM((hpg * SSM_HEADDIM, SSM_STATE), F32),
                        pltpu.VMEM((lb + 16, 768), F32)],
        compiler_params=pltpu.CompilerParams(dimension_semantics=("parallel", "parallel", "arbitrary")),
    )(u, u, u, u, u, conv_buf, conv_buf, conv_buf,
      p["conv_w"], p["conv_w"], p["conv_w"], p["conv_b"], p["conv_b"], p["conv_b"],
      p["dt_bias"], p["a_log"], p["d_skip"], p["norm_g"], s0)


def _merge_kernel(gate_ref, og_ref, or_ref, os_ref, x_ref, wg_ref, wr_ref, ws_ref, wo_ref,
                  g_ref, b_ref, o_ref):
    d = D_MODEL
    m = _sigmoid(gate_ref[:, 0:d]) * jnp.dot(og_ref[...], wg_ref[...], preferred_element_type=F32)
    m = m + _sigmoid(gate_ref[:, d:2 * d]) * jnp.dot(or_ref[...], wr_ref[...], preferred_element_type=F32)
    m = m + _sigmoid(gate_ref[:, 2 * d:3 * d]) * jnp.dot(os_ref[...], ws_ref[...], preferred_element_type=F32)
    y = DN_ALPHA * x_ref[...] + _mm(m, wo_ref[...])
    o_ref[...] = _layer_norm(y, g_ref[...], b_ref[...])


def _merge_call(u, og, orr, os_, x, wg, wr, ws, wo, g, b):
    t = x.shape[0]
    tm = min(t, 256)
    tok = lambda w: pl.BlockSpec((tm, w), lambda i: (i, 0))
    const = lambda r, c: pl.BlockSpec((r, c), lambda i: (0, 0))
    return pl.pallas_call(
        _merge_kernel,
        name="merge_outproj_ln",
        grid=(t // tm,),
        in_specs=[tok(3 * D_MODEL), tok(D_MODEL), tok(D_MODEL), tok(SSM_INNER), tok(D_MODEL),
                  const(D_MODEL, D_MODEL), const(D_MODEL, D_MODEL), const(SSM_INNER, D_MODEL),
                  const(D_MODEL, D_MODEL), const(1, D_MODEL), const(1, D_MODEL)],
        out_specs=tok(D_MODEL),
        out_shape=jax.ShapeDtypeStruct((t, D_MODEL), F32),
        compiler_params=pltpu.CompilerParams(dimension_semantics=("parallel",),
                                             vmem_limit_bytes=VMEM_LIMIT),
    )(u, og, orr, os_, x, wg, wr, ws, wo, g, b)


def _router_gates(xb, wrt_ref, rb_ref, key_ref):
    tm = xb.shape[0]
    per_group = N_EXPERTS // N_EXPERT_GROUPS
    s_t = _sigmoid(_mm_nt(wrt_ref[...], xb))
    sb = s_t + rb_ref[...]
    gscore = []
    for g in range(N_EXPERT_GROUPS):
        xg = sb[g * per_group:(g + 1) * per_group, :]
        m1 = jnp.max(xg, axis=0, keepdims=True)
        eq = xg == m1
        cnt = jnp.sum(eq.astype(F32), axis=0, keepdims=True)
        m2 = jnp.max(jnp.where(eq, -jnp.inf, xg), axis=0, keepdims=True)
        gscore.append(m1 + jnp.where(cnt >= 2.0, m1, m2))
    for g in range(N_EXPERT_GROUPS):
        rank = jnp.zeros((1, tm), F32)
        for g2 in range(N_EXPERT_GROUPS):
            if g2 == g:
                continue
            beats = (gscore[g2] >= gscore[g]) if g2 < g else (gscore[g2] > gscore[g])
            rank = rank + beats.astype(F32)
        keep = rank < float(TOPK_GROUPS)
        key_ref[g * per_group:(g + 1) * per_group, :] = jnp.where(
            keep, sb[g * per_group:(g + 1) * per_group, :], -jnp.inf)
    key = key_ref[...]
    eidx = lax.broadcasted_iota(jnp.int32, (N_EXPERTS, tm), 0)

    def body(ep, rank):
        rowv = key_ref[pl.ds(ep, 1), :]
        beats = (rowv > key) | ((rowv == key) & (ep < eidx))
        return rank + beats.astype(F32)

    rank = lax.fori_loop(0, N_EXPERTS, body, jnp.zeros((N_EXPERTS, tm), F32))
    w = jnp.where(rank < float(TOP_K), s_t, 0.0)
    return w / jnp.sum(w, axis=0, keepdims=True) * ROUTED_SCALE


def _moe_kernel(x_ref, wrt_ref, rb_ref, w1_ref, w3_ref, w2_ref, s1_ref, s3_ref, s2_ref,
                g_ref, b_ref, o_ref, xb_ref, gates_ref, key_ref, *, sub):
    e = pl.program_id(1)
    tm = x_ref.shape[0]

    @pl.when(e == 0)
    def _():
        xb_ref[...] = x_ref[...].astype(BF16)
        g_t = _router_gates(xb_ref[...], wrt_ref, rb_ref, key_ref)
        gates_ref[...] = jnp.transpose(jnp.concatenate([g_t, jnp.zeros_like(g_t)], axis=0))
        for s in range(tm // sub):
            rows = pl.ds(s * sub, sub)
            xs = xb_ref[rows, :]
            hid = _silu(jnp.dot(xs, s1_ref[...], preferred_element_type=F32)) * jnp.dot(
                xs, s3_ref[...], preferred_element_type=F32)
            o_ref[rows, :] = _mm(hid, s2_ref[...])

    lane = lax.broadcasted_iota(jnp.int32, (1, 128), 1)
    per_step = w1_ref.shape[0]
    for s in range(tm // sub):
        rows = pl.ds(s * sub, sub)
        xs = xb_ref[rows, :]
        acc = o_ref[rows, :]
        for j in range(per_step):
            gcol = jnp.sum(jnp.where(lane == e * per_step + j, gates_ref[rows, :], 0.0),
                           axis=1, keepdims=True)
            hid = _silu(jnp.dot(xs, w1_ref[j], preferred_element_type=F32)) * jnp.dot(
                xs, w3_ref[j], preferred_element_type=F32)
            acc = acc + _mm(hid * gcol, w2_ref[j])
        o_ref[rows, :] = acc

    @pl.when(e == pl.num_programs(1) - 1)
    def _():
        o_ref[...] = _layer_norm(DN_ALPHA * x_ref[...] + o_ref[...], g_ref[...], b_ref[...])


def _moe_call(h, wrt, rb, w1, w3, w2, s1, s3, s2, g, b):
    t = h.shape[0]
    tm = min(t, 2048)
    sub = min(tm, 512)
    per_step = 2
    const = lambda r, c: pl.BlockSpec((r, c), lambda i, e: (0, 0))
    return pl.pallas_call(
        functools.partial(_moe_kernel, sub=sub),
        name="moe_ln",
        grid=(t // tm, N_EXPERTS // per_step),
        in_specs=[pl.BlockSpec((tm, D_MODEL), lambda i, e: (i, 0)),
                  const(N_EXPERTS, D_MODEL), const(N_EXPERTS, 1),
                  pl.BlockSpec((per_step, D_MODEL, D_EXPERT), lambda i, e: (e, 0, 0)),
                  pl.BlockSpec((per_step, D_MODEL, D_EXPERT), lambda i, e: (e, 0, 0)),
                  pl.BlockSpec((per_step, D_EXPERT, D_MODEL), lambda i, e: (e, 0, 0)),
                  const(D_MODEL, D_EXPERT), const(D_MODEL, D_EXPERT), const(D_EXPERT, D_MODEL),
                  const(1, D_MODEL), const(1, D_MODEL)],
        out_specs=pl.BlockSpec((tm, D_MODEL), lambda i, e: (i, 0)),
        out_shape=jax.ShapeDtypeStruct((t, D_MODEL), F32),
        scratch_shapes=[pltpu.VMEM((tm, D_MODEL), BF16),
                        pltpu.VMEM((tm, 128), F32),
                        pltpu.VMEM((N_EXPERTS, tm), F32)],
        compiler_params=pltpu.CompilerParams(dimension_semantics=("parallel", "arbitrary"),
                                             vmem_limit_bytes=VMEM_LIMIT),
    )(h, wrt, rb, w1, w3, w2, s1, s3, s2, g, b)


def _prep_layer(w_in, gla_wa2, gla_ba, gla_norm_g, gla_wo,
                rwkv_mu, rwkv_w0, rwkv_w2, rwkv_a0, rwkv_a2, rwkv_g2, rwkv_k_k, rwkv_k_a, rwkv_r_k,
                rwkv_ln_g, rwkv_ln_b, rwkv_wo,
                ssm_conv_w, ssm_conv_b, ssm_dt_bias, ssm_a_log, ssm_d, ssm_norm_g, ssm_wo,
                w_out, ln1_g, ln1_b,
                w_router, router_bias, exp_w1, exp_w3, exp_w2, sh_w1, sh_w3, sh_w2, ln2_g, ln2_b):
    d = D_MODEL
    o_gla, o_rwkv = 3 * d, 3 * d + 3088
    o_ssm = o_rwkv + RWKV_PROJ
    zeros = lambda n: jnp.zeros((d, n), w_in.dtype)
    dt0 = o_ssm + SSM_INNER + SSM_CONV_DIM
    hpg = SSM_HEADS // SSM_GROUPS
    dt_cols = []
    for g in range(SSM_GROUPS):
        dt_cols += [w_in[:, dt0 + g * hpg:dt0 + (g + 1) * hpg], zeros(128 - hpg)]
    w_u = jnp.concatenate(
        [w_in[:, 0:3 * d], w_in[:, o_gla:o_gla + 3072], w_in[:, o_rwkv:o_rwkv + 3072],
         w_in[:, o_ssm:o_ssm + SSM_INNER + SSM_CONV_DIM], w_in[:, o_rwkv + 3072:o_rwkv + 3328],
         w_in[:, o_gla + 3072:o_gla + 3088], zeros(128 - GLA_RANK)] + dt_cols + [zeros(128)],
        axis=1).astype(BF16)
    assert w_u.shape[1] == N_U
    row = lambda v: v.reshape(1, -1)
    pad_groups = lambda v: jnp.pad(v.reshape(SSM_GROUPS, 1, hpg), ((0, 0), (0, 0), (0, 128 - hpg)))
    return dict(
        w_u=w_u,
        gla=dict(wa2=jnp.pad(gla_wa2, ((0, 128 - GLA_RANK), (0, 0))).astype(BF16),
                 ba=row(gla_ba), ng=row(gla_norm_g)),
        rwkv=dict(mu=row(rwkv_mu), w0=row(rwkv_w0), w2=rwkv_w2.astype(BF16), a0=row(rwkv_a0),
                  a2=rwkv_a2.astype(BF16), g2=rwkv_g2.astype(BF16), k_k=row(rwkv_k_k),
                  k_a=row(rwkv_k_a), r_k=row(rwkv_r_k), ln_g=row(rwkv_ln_g), ln_b=row(rwkv_ln_b)),
        ssm=dict(conv_w=ssm_conv_w, conv_b=row(ssm_conv_b), dt_bias=pad_groups(ssm_dt_bias),
                 a_log=pad_groups(ssm_a_log), d_skip=row(jnp.repeat(ssm_d, SSM_HEADDIM)),
                 norm_g=row(ssm_norm_g)),
        merge=(gla_wo.astype(BF16), rwkv_wo.astype(BF16), ssm_wo.astype(BF16), w_out.astype(BF16),
               row(ln1_g), row(ln1_b)),
        moe=(jnp.transpose(w_router).astype(BF16), router_bias.reshape(-1, 1),
             exp_w1.astype(BF16), exp_w3.astype(BF16), exp_w2.astype(BF16),
             sh_w1.astype(BF16), sh_w3.astype(BF16), sh_w2.astype(BF16), row(ln2_g), row(ln2_b)),
    )


def _layer(h, st_gla, st_rwkv, st_shift, st_ssm, st_conv, lp, b, l):
    u = _inproj_call(h, lp["w_u"])
    o_gla, s_gla = _gla_call(u, st_gla, lp["gla"]["wa2"], lp["gla"]["ba"], lp["gla"]["ng"], b, l)
    o_rwkv, s_rwkv = _rwkv_call(u, st_shift, st_rwkv, lp["rwkv"], b, l)
    o_ssm, s_ssm = _ssd_call(u, st_conv, st_ssm, lp["ssm"], b, l)
    h1 = _merge_call(u, o_gla, o_rwkv, o_ssm, h, *lp["merge"])
    h2 = _moe_call(h1, *lp["moe"])
    u3 = u.reshape(b, l, N_U)
    shift_new = jnp.concatenate([u3[:, l - 1:, OFF_RWKV:OFF_RWKV + 3072],
                                 u3[:, l - 1:, OFF_RLOW:OFF_RLOW + 256]], axis=-1)
    conv_new = u3[:, l - (SSM_CONV - 1):, OFF_SSM + SSM_INNER:OFF_SSM + SSM_INNER + SSM_CONV_DIM]
    return h2, s_gla, s_rwkv, shift_new, s_ssm, conv_new


def _zero_states(batch):
    return (jnp.zeros((batch, GLA_HEADS, GLA_DK, GLA_DV), F32),
            jnp.zeros((batch, RWKV_HEADS, RWKV_HEAD, RWKV_HEAD), F32),
            jnp.zeros((batch, 1, RWKV_PROJ), F32),
            jnp.zeros((batch, SSM_HEADS, SSM_HEADDIM, SSM_STATE), F32),
            jnp.zeros((batch, SSM_CONV - 1, SSM_CONV_DIM), F32))


def kernel(x_prompt, x_sample, state_gla, state_rwkv, state_rwkv_shift, state_ssm, state_ssm_conv,
           ln_in_g, ln_in_b, w_in, gla_wa2, gla_ba, gla_norm_g, gla_wo,
           rwkv_mu, rwkv_w0, rwkv_w2, rwkv_a0, rwkv_a2, rwkv_g2, rwkv_k_k, rwkv_k_a, rwkv_r_k,
           rwkv_ln_g, rwkv_ln_b, rwkv_wo,
           ssm_conv_w, ssm_conv_b, ssm_dt_bias, ssm_a_log, ssm_d, ssm_norm_g, ssm_wo,
           w_out, ln1_g, ln1_b,
           w_router, router_bias, exp_w1, exp_w3, exp_w2, sh_w1, sh_w3, sh_w2, ln2_g, ln2_b):
    layer_params = (w_in, gla_wa2, gla_ba, gla_norm_g, gla_wo,
                    rwkv_mu, rwkv_w0, rwkv_w2, rwkv_a0, rwkv_a2, rwkv_g2, rwkv_k_k, rwkv_k_a, rwkv_r_k,
                    rwkv_ln_g, rwkv_ln_b, rwkv_wo,
                    ssm_conv_w, ssm_conv_b, ssm_dt_bias, ssm_a_log, ssm_d, ssm_norm_g, ssm_wo,
                    w_out, ln1_g, ln1_b,
                    w_router, router_bias, exp_w1, exp_w3, exp_w2, sh_w1, sh_w3, sh_w2, ln2_g, ln2_b)
    depth = w_in.shape[0]
    bp, lp_, d = x_prompt.shape
    bs, ls, _ = x_sample.shape
    h_p = _ln_call(x_prompt.reshape(bp * lp_, d), ln_in_g, ln_in_b)
    h_s = _ln_call(x_sample.reshape(bs * ls, d), ln_in_g, ln_in_b)
    new_p, new_s = [], []
    for i in range(depth):
        lp = _prep_layer(*[t[i] for t in layer_params])
        h_p, *st_p = _layer(h_p, *_zero_states(bp), lp, bp, lp_)
        new_p.append(st_p)
        h_s, *st_s = _layer(h_s, state_gla[i], state_rwkv[i], state_rwkv_shift[i], state_ssm[i],
                            state_ssm_conv[i], lp, bs, ls)
        new_s.append(st_s)
    stack = lambda rows: tuple(jnp.stack(z) for z in zip(*rows))
    return (h_p.reshape(bp, lp_, d), h_s.reshape(bs, ls, d)) + stack(new_p) + stack(new_s)
```

```python
import functools

import jax
import jax.numpy as jnp
from jax import lax
from jax.experimental import pallas as pl
from jax.experimental.pallas import tpu as pltpu

F32 = jnp.float32
BF16 = jnp.bfloat16

D_MODEL = 1024
CHUNK = 64
GLA_HEADS, GLA_DK, GLA_DV, GLA_RANK = 4, 128, 256, 16
GLA_GATE_NORM = 16.0
GLA_NORM_EPS = 1e-5
RWKV_HEAD, RWKV_HEADS, RWKV_DIM = 64, 16, 1024
RWKV_PROJ = 3 * RWKV_DIM + 64 + 64 + 128
RWKV_GN_EPS = 64e-5
SSM_INNER, SSM_HEADDIM, SSM_HEADS, SSM_GROUPS, SSM_STATE, SSM_CONV = 2048, 64, 32, 4, 128, 4
SSM_CONV_DIM = SSM_INNER + 2 * SSM_GROUPS * SSM_STATE
SSM_NORM_EPS = 1e-5
N_EXPERTS, TOP_K, N_EXPERT_GROUPS, TOPK_GROUPS, D_EXPERT = 64, 8, 8, 4, 256
ROUTED_SCALE = 2.5
DEPTH = 2
DN_ALPHA = (2.0 * DEPTH) ** 0.25
LN_EPS = 1e-5

OFF_GATE, OFF_GLA, OFF_RWKV, OFF_SSM = 0, 3072, 6144, 9216
OFF_RLOW, OFF_GALOW, OFF_DT = 14336, 14592, 14720
N_U = 15360
VMEM_LIMIT = 56 * 1024 * 1024


def _mm(a, b):
    return jnp.dot(a.astype(BF16), b.astype(BF16), preferred_element_type=F32)


def _mm_nt(a, b):
    return lax.dot_general(a.astype(BF16), b.astype(BF16), (((1,), (1,)), ((), ())),
                           preferred_element_type=F32)


def _mm_tn(a, b):
    return lax.dot_general(a.astype(BF16), b.astype(BF16), (((0,), (0,)), ((), ())),
                           preferred_element_type=F32)


def _sigmoid(x):
    return 1.0 / (1.0 + jnp.exp(-x))


def _silu(x):
    return x * _sigmoid(x)


def _softplus(x):
    return jnp.maximum(x, 0.0) + jnp.log(1.0 + jnp.exp(-jnp.abs(x)))


def _layer_norm(x, g, b):
    mu = jnp.mean(x, -1, keepdims=True)
    xc = x - mu
    var = jnp.mean(xc * xc, -1, keepdims=True)
    return xc * lax.rsqrt(var + LN_EPS) * g + b


def _tril_mask(n, strict=False):
    r = lax.broadcasted_iota(jnp.int32, (n, n), 0)
    c = lax.broadcasted_iota(jnp.int32, (n, n), 1)
    return (r > c) if strict else (r >= c)


def _cumsum_rows(x, tril_bf16):
    x1 = x.astype(BF16)
    r1 = x - x1.astype(F32)
    x2 = r1.astype(BF16)
    x3 = (r1 - x2.astype(F32)).astype(BF16)
    dot = lambda p: jnp.dot(tril_bf16, p, preferred_element_type=F32)
    return dot(x1) + dot(x2) + dot(x3)


def _ln_kernel(x_ref, g_ref, b_ref, o_ref):
    o_ref[...] = _layer_norm(x_ref[...], g_ref[...], b_ref[...])


def _ln_call(x, g, b):
    t = x.shape[0]
    tm = min(t, 512)
    return pl.pallas_call(
        _ln_kernel,
        name="ln_in",
        grid=(t // tm,),
        in_specs=[pl.BlockSpec((tm, D_MODEL), lambda i: (i, 0)),
                  pl.BlockSpec((1, D_MODEL), lambda i: (0, 0)),
                  pl.BlockSpec((1, D_MODEL), lambda i: (0, 0))],
        out_specs=pl.BlockSpec((tm, D_MODEL), lambda i: (i, 0)),
        out_shape=jax.ShapeDtypeStruct((t, D_MODEL), F32),
        compiler_params=pltpu.CompilerParams(dimension_semantics=("parallel",)),
    )(x, g.reshape(1, -1), b.reshape(1, -1))


def _inproj_kernel(x_ref, w_ref, o_ref, xb_ref):
    @pl.when(pl.program_id(1) == 0)
    def _():
        xb_ref[...] = x_ref[...].astype(BF16)

    o_ref[...] = jnp.dot(xb_ref[...], w_ref[...], preferred_element_type=F32)


def _inproj_call(h, w):
    t, n = h.shape[0], w.shape[1]
    tm = min(t, 2048)
    tn = 1024
    return pl.pallas_call(
        _inproj_kernel,
        name="inproj",
        grid=(t // tm, n // tn),
        in_specs=[pl.BlockSpec((tm, D_MODEL), lambda i, j: (i, 0)),
                  pl.BlockSpec((D_MODEL, tn), lambda i, j: (0, j))],
        out_specs=pl.BlockSpec((tm, tn), lambda i, j: (i, j)),
        out_shape=jax.ShapeDtypeStruct((t, n), F32),
        scratch_shapes=[pltpu.VMEM((tm, D_MODEL), BF16)],
        compiler_params=pltpu.CompilerParams(dimension_semantics=("parallel", "arbitrary"),
                                             vmem_limit_bytes=VMEM_LIMIT),
    )(h, w)


def _gla_kernel(q_ref, k_ref, v_ref, r_ref, al_ref, wa2_ref, ba_ref, ng_ref, s0_ref,
                o_ref, sn_ref, s_ref, *, nc):
    i = pl.program_id(2)

    @pl.when(i == 0)
    def _():
        s_ref[...] = s0_ref[0, 0]

    tril = _tril_mask(CHUNK)
    tril_b = tril.astype(BF16)
    eye = (lax.broadcasted_iota(jnp.int32, (GLA_DK, GLA_DK), 0)
           == lax.broadcasted_iota(jnp.int32, (GLA_DK, GLA_DK), 1))
    chunks = range(nc)
    sl = [pl.ds(c * CHUNK, CHUNK) for c in chunks]
    z = _mm(al_ref[...], wa2_ref[...]) + ba_ref[...]
    la = -_softplus(-z) * (1.0 / GLA_GATE_NORM)
    g = [_cumsum_rows(la[c * CHUNK:(c + 1) * CHUNK], tril_b) for c in chunks]
    g_last = [g[c][CHUNK - 1:CHUNK, :] for c in chunks]
    q_dec = [q_ref[sl[c], :] * jnp.exp(g[c]) * (GLA_DK ** -0.5) for c in chunks]
    k_inv = [k_ref[sl[c], :] * jnp.exp(-g[c]) for c in chunks]
    k_end = [k_ref[sl[c], :] * jnp.exp(g_last[c] - g[c]) for c in chunks]
    scores = [jnp.where(tril, _mm_nt(q_dec[c], k_inv[c]), 0.0) for c in chunks]
    kv = [_mm_tn(k_end[c], v_ref[sl[c], :]) for c in chunks]
    o_intra = [_mm(scores[c], v_ref[sl[c], :]) for c in chunks]
    dec_col = [jnp.sum(jnp.where(eye, jnp.exp(g_last[c]), 0.0), axis=1, keepdims=True) for c in chunks]
    s_cur = s_ref[...]
    for c in chunks:
        o = o_intra[c] + _mm(q_dec[c], s_cur)
        s_cur = s_cur * dec_col[c] + kv[c]
        o = o * lax.rsqrt(jnp.mean(o * o, -1, keepdims=True) + GLA_NORM_EPS) * ng_ref[...]
        o_ref[sl[c], :] = (o * _silu(r_ref[sl[c], :])).astype(o_ref.dtype)
    s_ref[...] = s_cur

    @pl.when(i == pl.num_programs(2) - 1)
    def _():
        sn_ref[0, 0] = s_ref[...]


def _gla_call(u, s0, wa2p, ba, ng, b, l):
    lb = min(l, 512)
    nb = l // lb
    row = lambda bb, h, i: bb * nb + i
    spec = lambda w, off: pl.BlockSpec((lb, w), lambda bb, h, i: (row(bb, h, i), off + h))
    return pl.pallas_call(
        functools.partial(_gla_kernel, nc=lb // CHUNK),
        name="gla_mixer",
        grid=(b, GLA_HEADS, nb),
        in_specs=[spec(128, OFF_GLA // 128), spec(128, (OFF_GLA + 512) // 128),
                  spec(256, (OFF_GLA + 1024) // 256), spec(256, (OFF_GLA + 2048) // 256),
                  pl.BlockSpec((lb, 128), lambda bb, h, i: (row(bb, h, i), OFF_GALOW // 128)),
                  pl.BlockSpec((128, 128), lambda bb, h, i: (0, h)),
                  pl.BlockSpec((1, 128), lambda bb, h, i: (0, h)),
                  pl.BlockSpec((1, 256), lambda bb, h, i: (0, 0)),
                  pl.BlockSpec((1, 1, GLA_DK, GLA_DV), lambda bb, h, i: (bb, h, 0, 0))],
        out_specs=[pl.BlockSpec((lb, 256), lambda bb, h, i: (row(bb, h, i), h)),
                   pl.BlockSpec((1, 1, GLA_DK, GLA_DV), lambda bb, h, i: (bb, h, 0, 0))],
        out_shape=[jax.ShapeDtypeStruct((b * l, GLA_HEADS * GLA_DV), BF16),
                   jax.ShapeDtypeStruct((b, GLA_HEADS, GLA_DK, GLA_DV), F32)],
        scratch_shapes=[pltpu.VMEM((GLA_DK, GLA_DV), F32)],
        compiler_params=pltpu.CompilerParams(dimension_semantics=("parallel", "parallel", "arbitrary")),
    )(u, u, u, u, u, wa2p, ba, ng, s0)


def _half_sum(x, lo):
    s0 = jnp.sum(jnp.where(lo, x, 0.0), -1, keepdims=True)
    s1 = jnp.sum(jnp.where(lo, 0.0, x), -1, keepdims=True)
    return jnp.where(lo, s0, s1)


def _rwkv_kernel(r_ref, k_ref, v_ref, low_ref, shr_ref, shk_ref, shv_ref, shl_ref,
                 mur_ref, muk_ref, muv_ref, mul_ref, w0_ref, w2_ref, a0_ref, a2_ref, g2_ref,
                 kk_ref, ka_ref, rk_ref, lng_ref, lnb_ref, s0_ref,
                 o_ref, sn_ref, s_ref, sh_ref, *, nc):
    i = pl.program_id(2)
    lb = nc * CHUNK
    hd = RWKV_HEAD

    @pl.when(i == 0)
    def _():
        zero = jnp.zeros((hd, hd), F32)
        s_ref[...] = jnp.concatenate([jnp.concatenate([s0_ref[0, 0], zero], axis=1),
                                      jnp.concatenate([zero, s0_ref[0, 1]], axis=1)], axis=0)
        sh_ref[7:8, 0:128] = shr_ref[0]
        sh_ref[7:8, 128:256] = shk_ref[0]
        sh_ref[7:8, 256:384] = shv_ref[0]
        sh_ref[7:8, 384:640] = shl_ref[0]

    sh_ref[8:8 + lb, 0:128] = r_ref[...]
    sh_ref[8:8 + lb, 128:256] = k_ref[...]
    sh_ref[8:8 + lb, 256:384] = v_ref[...]
    sh_ref[8:8 + lb, 384:640] = low_ref[...]

    tril_b = _tril_mask(CHUNK).astype(BF16)
    row = lax.broadcasted_iota(jnp.int32, (CHUNK, 128), 0)
    pos = jnp.bitwise_and(lax.broadcasted_iota(jnp.int32, (CHUNK, 128), 1), jnp.int32(hd - 1))
    tril, stril, eye_f = row >= pos, row > pos, (row == pos).astype(F32)
    lo = lax.broadcasted_iota(jnp.int32, (1, 128), 1) < hd

    prep = []
    for c in range(nc):
        cur = sh_ref[8 + c * CHUNK:8 + (c + 1) * CHUNK, :]
        prev = sh_ref[7 + c * CHUNK:7 + (c + 1) * CHUNK, :]
        shift = lambda a, b, mu: a + (b - a) * mu
        r = shift(cur[:, 0:128], prev[:, 0:128], mur_ref[...])
        k = shift(cur[:, 128:256], prev[:, 128:256], muk_ref[...])
        v = shift(cur[:, 256:384], prev[:, 256:384], muv_ref[...])
        low = shift(cur[:, 384:640], prev[:, 384:640], mul_ref[...])
        zw = w0_ref[...] + _mm(jnp.tanh(low[:, 0:64]), w2_ref[...])
        lw = -jnp.exp(-_softplus(-zw) - 0.5)
        a = _sigmoid(a0_ref[...] + _mm(low[:, 64:128], a2_ref[...]))
        gate = _mm(_sigmoid(low[:, 128:256]), g2_ref[...])
        kk = k * kk_ref[...]
        kk = kk * lax.rsqrt(jnp.maximum(_half_sum(kk * kk, lo), 1e-24))
        k2 = k * (1.0 + (a - 1.0) * ka_ref[...])
        lg = _cumsum_rows(lw, tril_b)
        e_neg = jnp.exp(-lg)
        prep.append(dict(a=-kk * jnp.exp(lg - lw), b=kk * a * e_neg, k=k2 * e_neg, r=r * jnp.exp(lg),
                         v=v, gam=jnp.exp(lg[CHUNK - 1:CHUNK, :]), gate=gate,
                         bonus=_half_sum(r * k2 * rk_ref[...], lo) * v))

    chunks = range(nc)
    bzero = jnp.zeros((), BF16)

    def bd(x):
        xb = x.astype(BF16)
        return jnp.concatenate([jnp.where(lo, xb, bzero), jnp.where(lo, bzero, xb)], axis=0)

    cast = lambda x: x.astype(BF16)
    dot = lambda a, b: jnp.dot(a, b, preferred_element_type=F32)
    dot_nt = lambda a, b: lax.dot_general(a, b, (((1,), (1,)), ((), ())), preferred_element_type=F32)
    dot_tn = lambda a, b: lax.dot_general(a, b, (((0,), (0,)), ((), ())), preferred_element_type=F32)
    get = lambda name: [prep[c][name] for c in chunks]
    at, bt, kt, rt, vv, gam = get("a"), get("b"), get("k"), get("r"), get("v"), get("gam")
    bd_b, bd_k, bd_v, bd_a = ([bd(x[c]) for c in chunks] for x in (bt, kt, vv, at))
    ar = [cast(jnp.concatenate([at[c], rt[c]], axis=0)) for c in chunks]
    pb = [dot_nt(ar[c], bd_b[c]) for c in chunks]
    pk = [dot_nt(ar[c], bd_k[c]) for c in chunks]
    n_ab = [jnp.where(stril, pb[c][:CHUNK], 0.0) for c in chunks]
    m_rb = [jnp.where(tril, pb[c][CHUNK:], 0.0) for c in chunks]
    n_ak = [jnp.where(stril, pk[c][:CHUNK], 0.0) for c in chunks]
    m_rk = [jnp.where(tril, pk[c][CHUNK:], 0.0) for c in chunks]
    t_inv = [eye_f + n_ab[c] for c in chunks]
    pw = n_ab
    bd_pw = [bd(pw[c]) for c in chunks]
    for _ in range(5):
        pw = [dot(cast(pw[c]), bd_pw[c]) for c in chunks]
        bd_pw = [bd(pw[c]) for c in chunks]
        t_inv = [t_inv[c] + dot(cast(t_inv[c]), bd_pw[c]) for c in chunks]
    akv = [dot(cast(n_ak[c]), bd_v[c]) for c in chunks]
    au = [dot(cast(t_inv[c]), jnp.concatenate([bd_a[c], bd(akv[c])], axis=1)) for c in chunks]
    a_hat = [au[c][:, 0:128] for c in chunks]
    u0 = [au[c][:, 128:256] for c in chunks]
    bd_ah = [bd(a_hat[c]) for c in chunks]
    bd_u0v = [jnp.concatenate([bd(u0[c]), bd_v[c]], axis=0) for c in chunks]
    g_lr = [dot_tn(bd_ah[c], bd_b[c]) * gam[c] for c in chunks]
    c0 = [dot_tn(bd_u0v[c], jnp.concatenate([bd_b[c], bd_k[c]], axis=0)) * gam[c] for c in chunks]
    r_hat = [rt[c] + dot(cast(m_rb[c]), bd_ah[c]) for c in chunks]
    y0 = [dot(cast(jnp.concatenate([m_rb[c], m_rk[c]], axis=1)), bd_u0v[c]) for c in chunks]

    s_cur = s_ref[...]
    for c in chunks:
        s_b = cast(s_cur)
        y = dot_nt(cast(r_hat[c]), s_b) + y0[c]
        s_cur = s_cur * gam[c] + dot(s_b, cast(g_lr[c])) + c0[c]
        mean = _half_sum(y, lo) * (1.0 / hd)
        yc = y - mean
        var = _half_sum(yc * yc, lo) * (1.0 / hd)
        y = yc * lax.rsqrt(var + RWKV_GN_EPS) * lng_ref[...] + lnb_ref[...]
        o_ref[c * CHUNK:(c + 1) * CHUNK, :] = ((y + prep[c]["bonus"]) * prep[c]["gate"]).astype(o_ref.dtype)
    s_ref[...] = s_cur

    sh_ref[7:8, :] = sh_ref[7 + lb:8 + lb, :]

    @pl.when(i == pl.num_programs(2) - 1)
    def _():
        sn_ref[0, 0] = s_ref[0:hd, 0:hd]
        sn_ref[0, 1] = s_ref[hd:2 * hd, hd:2 * hd]


def _rwkv_call(u, shift_buf, s0, p, b, l):
    lb = min(l, 1024)
    nb = l // lb
    row = lambda bb, h, i: bb * nb + i
    ublk = lambda off: pl.BlockSpec((lb, 128), lambda bb, h, i: (row(bb, h, i), off // 128 + h))
    sblk = lambda off: pl.BlockSpec((1, 1, 128), lambda bb, h, i: (bb, 0, off // 128 + h))
    vec = lambda: pl.BlockSpec((1, 128), lambda bb, h, i: (0, h))
    const = lambda shape: pl.BlockSpec(shape, lambda bb, h, i: (0,) * len(shape))
    return pl.pallas_call(
        functools.partial(_rwkv_kernel, nc=lb // CHUNK),
        name="rwkv_mixer",
        grid=(b, RWKV_HEADS // 2, nb),
        in_specs=[ublk(OFF_RWKV), ublk(OFF_RWKV + 1024), ublk(OFF_RWKV + 2048),
                  pl.BlockSpec((lb, 256), lambda bb, h, i: (row(bb, h, i), OFF_RLOW // 256)),
                  sblk(0), sblk(1024), sblk(2048),
                  pl.BlockSpec((1, 1, 256), lambda bb, h, i: (bb, 0, 3072 // 256)),
                  vec(), pl.BlockSpec((1, 128), lambda bb, h, i: (0, 8 + h)),
                  pl.BlockSpec((1, 128), lambda bb, h, i: (0, 16 + h)),
                  pl.BlockSpec((1, 256), lambda bb, h, i: (0, 3072 // 256)),
                  vec(), pl.BlockSpec((64, 128), lambda bb, h, i: (0, h)),
                  vec(), pl.BlockSpec((64, 128), lambda bb, h, i: (0, h)),
                  pl.BlockSpec((128, 128), lambda bb, h, i: (0, h)),
                  vec(), vec(), vec(), vec(), vec(),
                  pl.BlockSpec((1, 2, RWKV_HEAD, RWKV_HEAD), lambda bb, h, i: (bb, h, 0, 0))],
        out_specs=[pl.BlockSpec((lb, 128), lambda bb, h, i: (row(bb, h, i), h)),
                   pl.BlockSpec((1, 2, RWKV_HEAD, RWKV_HEAD), lambda bb, h, i: (bb, h, 0, 0))],
        out_shape=[jax.ShapeDtypeStruct((b * l, RWKV_DIM), BF16),
                   jax.ShapeDtypeStruct((b, RWKV_HEADS, RWKV_HEAD, RWKV_HEAD), F32)],
        scratch_shapes=[pltpu.VMEM((2 * RWKV_HEAD, 2 * RWKV_HEAD), F32),
                        pltpu.VMEM((lb + 8, 640), F32)],
        compiler_params=pltpu.CompilerParams(dimension_semantics=("parallel", "parallel", "arbitrary")),
    )(u, u, u, u, shift_buf, shift_buf, shift_buf, shift_buf,
      p["mu"], p["mu"], p["mu"], p["mu"], p["w0"], p["w2"], p["a0"], p["a2"], p["g2"],
      p["k_k"], p["k_a"], p["r_k"], p["ln_g"], p["ln_b"], s0)


def _ssd_kernel(z_ref, x_ref, bm_ref, cm_ref, dt_ref, cbx_ref, cbb_ref, cbc_ref,
                cwx_ref, cwb_ref, cwc_ref, cbiasx_ref, cbiasb_ref, cbiasc_ref,
                dtb_ref, alog_ref, dsk_ref, ng_ref, s0_ref,
                o_ref, sn_ref, s_ref, full_ref, *, nc):
    i = pl.program_id(2)
    lb = nc * CHUNK
    hpg = SSM_HEADS // SSM_GROUPS
    p = SSM_HEADDIM
    wx = hpg * p

    @pl.when(i == 0)
    def _():
        s_ref[...] = s0_ref[0].reshape(hpg * p, SSM_STATE)
        full_ref[5:8, 0:wx] = cbx_ref[0]
        full_ref[5:8, wx:wx + 128] = cbb_ref[0]
        full_ref[5:8, wx + 128:wx + 256] = cbc_ref[0]

    rnd = lambda a: a.astype(BF16).astype(F32)

    @pl.when(i == 0)
    def _():
        full_ref[5:8, :] = rnd(full_ref[5:8, :])
        full_ref[8 + lb:16 + lb, :] = jnp.zeros((8, wx + 256), F32)
        full_ref[0:5, :] = jnp.zeros((5, wx + 256), F32)

    full_ref[8:8 + lb, 0:wx] = rnd(x_ref[...])
    full_ref[8:8 + lb, wx:wx + 128] = rnd(bm_ref[...])
    full_ref[8:8 + lb, wx + 128:wx + 256] = rnd(cm_ref[...])

    tril = _tril_mask(CHUNK)
    tril_b = tril.astype(BF16)
    cw = jnp.concatenate([cwx_ref[...], cwb_ref[...], cwc_ref[...]], axis=1)
    cbias = jnp.concatenate([cbiasx_ref[...], cbiasb_ref[...], cbiasc_ref[...]], axis=1)
    a_row = -jnp.exp(alog_ref[0])

    iota = lambda shape, d: lax.broadcasted_iota(jnp.int32, shape, d)
    head_of = lambda idx: lax.shift_right_logical(idx, jnp.int32(6))
    expand = (iota((128, wx), 0) == head_of(iota((128, wx), 1))).astype(BF16)
    pos = jnp.bitwise_and(iota((CHUNK, wx), 1), jnp.int32(CHUNK - 1))
    tril_t = iota((CHUNK, wx), 0) >= pos
    diag_t = iota((CHUNK, wx), 0) == pos
    blockdiag = head_of(iota((wx, wx), 0)) == head_of(iota((wx, wx), 1))

    def split3(x):
        x1 = x.astype(BF16)
        r1 = x - x1.astype(F32)
        x2 = r1.astype(BF16)
        return x1, x2, (r1 - x2.astype(F32)).astype(BF16)

    dot = lambda a, b: jnp.dot(a, b, preferred_element_type=F32)
    per_head = lambda x: sum(dot(piece, expand) for piece in split3(x))

    chunks = range(nc)
    xs, bm, cm, dtv, acum = [], [], [], [], []
    win = CHUNK + 16
    shifts = [(iota((CHUNK, win), 1) == iota((CHUNK, win), 0) + jnp.int32(5 + j)).astype(BF16)
              for j in range(SSM_CONV - 1)]
    for c in chunks:
        base = c * CHUNK
        window = full_ref[base:base + win, :].astype(BF16)
        conv = cbias + cw[SSM_CONV - 1:SSM_CONV, :] * full_ref[8 + base:8 + base + CHUNK, :]
        for j in range(SSM_CONV - 1):
            conv = conv + cw[j:j + 1, :] * dot(shifts[j], window)
        conv = _silu(conv)
        xs.append(conv[:, 0:wx])
        bm.append(conv[:, wx:wx + 128])
        cm.append(conv[:, wx + 128:wx + 256])
        dtv.append(_softplus(dt_ref[base:base + CHUNK, :] + dtb_ref[0]))
    acum = [_cumsum_rows(dtv[c] * a_row, tril_b) for c in chunks]
    a_col = [per_head(acum[c]) for c in chunks]
    a_row_e = [jnp.sum(jnp.where(diag_t, a_col[c], 0.0), axis=0, keepdims=True)
               for c in chunks]
    dt_e = [per_head(dtv[c]) for c in chunks]
    cb_t = [_mm_nt(cm[c], jnp.concatenate([bm[c]] * hpg, axis=0)) for c in chunks]
    dec = [jnp.where(tril_t, jnp.exp(jnp.where(tril_t, a_col[c] - a_row_e[c], 0.0)), 0.0) for c in chunks]
    xdt = [xs[c] * dt_e[c] for c in chunks]
    xdt_bd = [jnp.where(blockdiag, jnp.concatenate([xdt[c].astype(BF16)] * hpg, axis=0),
                        jnp.zeros((), BF16)) for c in chunks]
    y_diag = [dot((cb_t[c] * dec[c]).astype(BF16), xdt_bd[c]) for c in chunks]
    a_last = [a_col[c][CHUNK - 1:CHUNK, :] for c in chunks]
    st = [_mm_tn(xdt[c] * jnp.exp(a_last[c] - a_col[c]), bm[c]) for c in chunks]
    e_end = [jnp.exp(acum[c][CHUNK - 1:CHUNK, :]) for c in chunks]
    e_in = [jnp.exp(a_col[c]) for c in chunks]
    s_cur = s_ref[...]
    for c in chunks:
        base = c * CHUNK
        y = y_diag[c] + _mm_nt(cm[c], s_cur) * e_in[c] + xs[c] * dsk_ref[...]
        s_cur = jnp.concatenate([s_cur[h * p:(h + 1) * p, :] * e_end[c][:, h:h + 1] for h in range(hpg)],
                                axis=0) + st[c]
        y = y * _silu(z_ref[base:base + CHUNK, :])
        y = y * lax.rsqrt(jnp.mean(y * y, -1, keepdims=True) + SSM_NORM_EPS) * ng_ref[...]
        o_ref[base:base + CHUNK, :] = y.astype(o_ref.dtype)
    s_ref[...] = s_cur

    full_ref[5:8, :] = full_ref[5 + lb:8 + lb, :]

    @pl.when(i == pl.num_programs(2) - 1)
    def _():
        sn_ref[0] = s_ref[...].reshape(hpg, p, SSM_STATE)


def _ssd_call(u, conv_buf, s0, p, b, l):
    lb = min(l, 256)
    nb = l // lb
    hpg = SSM_HEADS // SSM_GROUPS
    row = lambda bb, g, i: bb * nb + i
    ublk = lambda w, off: pl.BlockSpec((lb, w), lambda bb, g, i: (row(bb, g, i), off // w + g))
    gvec = lambda w, off: pl.BlockSpec((1, w), lambda bb, g, i: (0, off // w + g))
    gtap = lambda w, off: pl.BlockSpec((SSM_CONV, w), lambda bb, g, i: (0, off // w + g))
    gbuf = lambda w, off: pl.BlockSpec((1, SSM_CONV - 1, w), lambda bb, g, i: (bb, 0, off // w + g))
    return pl.pallas_call(
        functools.partial(_ssd_kernel, nc=lb // CHUNK),
        name="ssd_mixer",
        grid=(b, SSM_GROUPS, nb),
        in_specs=[ublk(512, OFF_SSM), ublk(512, OFF_SSM + 2048),
                  ublk(128, OFF_SSM + 4096), ublk(128, OFF_SSM + 4608), ublk(128, OFF_DT),
                  gbuf(512, 0), gbuf(128, 2048), gbuf(128, 2560),
                  gtap(512, 0), gtap(128, 2048), gtap(128, 2560),
                  gvec(512, 0), gvec(128, 2048), gvec(128, 2560),
                  pl.BlockSpec((1, 1, 128), lambda bb, g, i: (g, 0, 0)),
                  pl.BlockSpec((1, 1, 128), lambda bb, g, i: (g, 0, 0)),
                  gvec(512, 0), gvec(512, 0),
                  pl.BlockSpec((1, hpg, SSM_HEADDIM, SSM_STATE), lambda bb, g, i: (bb, g, 0, 0))],
        out_specs=[pl.BlockSpec((lb, 512), lambda bb, g, i: (row(bb, g, i), g)),
                   pl.BlockSpec((1, hpg, SSM_HEADDIM, SSM_STATE), lambda bb, g, i: (bb, g, 0, 0))],
        out_shape=[jax.ShapeDtypeStruct((b * l, SSM_INNER), BF16),
                   jax.ShapeDtypeStruct((b, SSM_HEADS, SSM_HEADDIM, SSM_STATE), F32)],
        scratch_shapes=[pltpu.VMEM((hpg * SSM_HEADDIM, SSM_STATE), F32),
                        pltpu.VMEM((lb + 16, 768), F32)],
        compiler_params=pltpu.CompilerParams(dimension_semantics=("parallel", "parallel", "arbitrary")),
    )(u, u, u, u, u, conv_buf, conv_buf, conv_buf,
      p["conv_w"], p["conv_w"], p["conv_w"], p["conv_b"], p["conv_b"], p["conv_b"],
      p["dt_bias"], p["a_log"], p["d_skip"], p["norm_g"], s0)


def _merge_kernel(gate_ref, og_ref, or_ref, os_ref, x_ref, wg_ref, wr_ref, ws_ref, wo_ref,
                  g_ref, b_ref, o_ref):
    d = D_MODEL
    m = _sigmoid(gate_ref[:, 0:d]) * jnp.dot(og_ref[...], wg_ref[...], preferred_element_type=F32)
    m = m + _sigmoid(gate_ref[:, d:2 * d]) * jnp.dot(or_ref[...], wr_ref[...], preferred_element_type=F32)
    m = m + _sigmoid(gate_ref[:, 2 * d:3 * d]) * jnp.dot(os_ref[...], ws_ref[...], preferred_element_type=F32)
    y = DN_ALPHA * x_ref[...] + _mm(m, wo_ref[...])
    o_ref[...] = _layer_norm(y, g_ref[...], b_ref[...])


def _merge_call(u, og, orr, os_, x, wg, wr, ws, wo, g, b):
    t = x.shape[0]
    tm = min(t, 256)
    tok = lambda w: pl.BlockSpec((tm, w), lambda i: (i, 0))
    const = lambda r, c: pl.BlockSpec((r, c), lambda i: (0, 0))
    return pl.pallas_call(
        _merge_kernel,
        name="merge_outproj_ln",
        grid=(t // tm,),
        in_specs=[tok(3 * D_MODEL), tok(D_MODEL), tok(D_MODEL), tok(SSM_INNER), tok(D_MODEL),
                  const(D_MODEL, D_MODEL), const(D_MODEL, D_MODEL), const(SSM_INNER, D_MODEL),
                  const(D_MODEL, D_MODEL), const(1, D_MODEL), const(1, D_MODEL)],
        out_specs=tok(D_MODEL),
        out_shape=jax.ShapeDtypeStruct((t, D_MODEL), F32),
        compiler_params=pltpu.CompilerParams(dimension_semantics=("parallel",),
                                             vmem_limit_bytes=VMEM_LIMIT),
    )(u, og, orr, os_, x, wg, wr, ws, wo, g, b)


def _router_gates(xb, wrt_ref, rb_ref, key_ref):
    tm = xb.shape[0]
    per_group = N_EXPERTS // N_EXPERT_GROUPS
    s_t = _sigmoid(_mm_nt(wrt_ref[...], xb))
    sb = s_t + rb_ref[...]
    gscore = []
    for g in range(N_EXPERT_GROUPS):
        xg = sb[g * per_group:(g + 1) * per_group, :]
        m1 = jnp.max(xg, axis=0, keepdims=True)
        eq = xg == m1
        cnt = jnp.sum(eq.astype(F32), axis=0, keepdims=True)
        m2 = jnp.max(jnp.where(eq, -jnp.inf, xg), axis=0, keepdims=True)
        gscore.append(m1 + jnp.where(cnt >= 2.0, m1, m2))
    for g in range(N_EXPERT_GROUPS):
        rank = jnp.zeros((1, tm), F32)
        for g2 in range(N_EXPERT_GROUPS):
            if g2 == g:
                continue
            beats = (gscore[g2] >= gscore[g]) if g2 < g else (gscore[g2] > gscore[g])
            rank = rank + beats.astype(F32)
        keep = rank < float(TOPK_GROUPS)
        key_ref[g * per_group:(g + 1) * per_group, :] = jnp.where(
            keep, sb[g * per_group:(g + 1) * per_group, :], -jnp.inf)
    key = key_ref[...]
    eidx = lax.broadcasted_iota(jnp.int32, (N_EXPERTS, tm), 0)

    def body(ep, rank):
        rowv = key_ref[pl.ds(ep, 1), :]
        beats = (rowv > key) | ((rowv == key) & (ep < eidx))
        return rank + beats.astype(F32)

    rank = lax.fori_loop(0, N_EXPERTS, body, jnp.zeros((N_EXPERTS, tm), F32))
    w = jnp.where(rank < float(TOP_K), s_t, 0.0)
    return w / jnp.sum(w, axis=0, keepdims=True) * ROUTED_SCALE


def _moe_kernel(x_ref, wrt_ref, rb_ref, w1_ref, w3_ref, w2_ref, s1_ref, s3_ref, s2_ref,
                g_ref, b_ref, o_ref, xb_ref, gates_ref, key_ref, *, sub):
    e = pl.program_id(1)
    tm = x_ref.shape[0]

    @pl.when(e == 0)
    def _():
        xb_ref[...] = x_ref[...].astype(BF16)
        g_t = _router_gates(xb_ref[...], wrt_ref, rb_ref, key_ref)
        gates_ref[...] = jnp.transpose(jnp.concatenate([g_t, jnp.zeros_like(g_t)], axis=0))
        for s in range(tm // sub):
            rows = pl.ds(s * sub, sub)
            xs = xb_ref[rows, :]
            hid = _silu(jnp.dot(xs, s1_ref[...], preferred_element_type=F32)) * jnp.dot(
                xs, s3_ref[...], preferred_element_type=F32)
            o_ref[rows, :] = _mm(hid, s2_ref[...])

    lane = lax.broadcasted_iota(jnp.int32, (1, 128), 1)
    per_step = w1_ref.shape[0]
    for s in range(tm // sub):
        rows = pl.ds(s * sub, sub)
        xs = xb_ref[rows, :]
        acc = o_ref[rows, :]
        for j in range(per_step):
            gcol = jnp.sum(jnp.where(lane == e * per_step + j, gates_ref[rows, :], 0.0),
                           axis=1, keepdims=True)
            hid = _silu(jnp.dot(xs, w1_ref[j], preferred_element_type=F32)) * jnp.dot(
                xs, w3_ref[j], preferred_element_type=F32)
            acc = acc + _mm(hid * gcol, w2_ref[j])
        o_ref[rows, :] = acc

    @pl.when(e == pl.num_programs(1) - 1)
    def _():
        o_ref[...] = _layer_norm(DN_ALPHA * x_ref[...] + o_ref[...], g_ref[...], b_ref[...])


def _moe_call(h, wrt, rb, w1, w3, w2, s1, s3, s2, g, b):
    t = h.shape[0]
    tm = min(t, 2048)
    sub = min(tm, 1024)
    per_step = 4
    const = lambda r, c: pl.BlockSpec((r, c), lambda i, e: (0, 0))
    return pl.pallas_call(
        functools.partial(_moe_kernel, sub=sub),
        name="moe_ln",
        grid=(t // tm, N_EXPERTS // per_step),
        in_specs=[pl.BlockSpec((tm, D_MODEL), lambda i, e: (i, 0), pipeline_mode=pl.Buffered(1)),
                  const(N_EXPERTS, D_MODEL), const(N_EXPERTS, 1),
                  pl.BlockSpec((per_step, D_MODEL, D_EXPERT), lambda i, e: (e, 0, 0)),
                  pl.BlockSpec((per_step, D_MODEL, D_EXPERT), lambda i, e: (e, 0, 0)),
                  pl.BlockSpec((per_step, D_EXPERT, D_MODEL), lambda i, e: (e, 0, 0)),
                  const(D_MODEL, D_EXPERT), const(D_MODEL, D_EXPERT), const(D_EXPERT, D_MODEL),
                  const(1, D_MODEL), const(1, D_MODEL)],
        out_specs=pl.BlockSpec((tm, D_MODEL), lambda i, e: (i, 0), pipeline_mode=pl.Buffered(1)),
        out_shape=jax.ShapeDtypeStruct((t, D_MODEL), F32),
        scratch_shapes=[pltpu.VMEM((tm, D_MODEL), BF16),
                        pltpu.VMEM((tm, 128), F32),
                        pltpu.VMEM((N_EXPERTS, tm), F32)],
        compiler_params=pltpu.CompilerParams(dimension_semantics=("parallel", "arbitrary"),
                                             vmem_limit_bytes=VMEM_LIMIT),
    )(h, wrt, rb, w1, w3, w2, s1, s3, s2, g, b)


def _prep_layer(w_in, gla_wa2, gla_ba, gla_norm_g, gla_wo,
                rwkv_mu, rwkv_w0, rwkv_w2, rwkv_a0, rwkv_a2, rwkv_g2, rwkv_k_k, rwkv_k_a, rwkv_r_k,
                rwkv_ln_g, rwkv_ln_b, rwkv_wo,
                ssm_conv_w, ssm_conv_b, ssm_dt_bias, ssm_a_log, ssm_d, ssm_norm_g, ssm_wo,
                w_out, ln1_g, ln1_b,
                w_router, router_bias, exp_w1, exp_w3, exp_w2, sh_w1, sh_w3, sh_w2, ln2_g, ln2_b):
    d = D_MODEL
    o_gla, o_rwkv = 3 * d, 3 * d + 3088
    o_ssm = o_rwkv + RWKV_PROJ
    zeros = lambda n: jnp.zeros((d, n), w_in.dtype)
    dt0 = o_ssm + SSM_INNER + SSM_CONV_DIM
    hpg = SSM_HEADS // SSM_GROUPS
    dt_cols = []
    for g in range(SSM_GROUPS):
        dt_cols += [w_in[:, dt0 + g * hpg:dt0 + (g + 1) * hpg], zeros(128 - hpg)]
    w_u = jnp.concatenate(
        [w_in[:, 0:3 * d], w_in[:, o_gla:o_gla + 3072], w_in[:, o_rwkv:o_rwkv + 3072],
         w_in[:, o_ssm:o_ssm + SSM_INNER + SSM_CONV_DIM], w_in[:, o_rwkv + 3072:o_rwkv + 3328],
         w_in[:, o_gla + 3072:o_gla + 3088], zeros(128 - GLA_RANK)] + dt_cols + [zeros(128)],
        axis=1).astype(BF16)
    assert w_u.shape[1] == N_U
    row = lambda v: v.reshape(1, -1)
    pad_groups = lambda v: jnp.pad(v.reshape(SSM_GROUPS, 1, hpg), ((0, 0), (0, 0), (0, 128 - hpg)))
    return dict(
        w_u=w_u,
        gla=dict(wa2=jnp.pad(gla_wa2, ((0, 128 - GLA_RANK), (0, 0))).astype(BF16),
                 ba=row(gla_ba), ng=row(gla_norm_g)),
        rwkv=dict(mu=row(rwkv_mu), w0=row(rwkv_w0), w2=rwkv_w2.astype(BF16), a0=row(rwkv_a0),
                  a2=rwkv_a2.astype(BF16), g2=rwkv_g2.astype(BF16), k_k=row(rwkv_k_k),
                  k_a=row(rwkv_k_a), r_k=row(rwkv_r_k), ln_g=row(rwkv_ln_g), ln_b=row(rwkv_ln_b)),
        ssm=dict(conv_w=ssm_conv_w, conv_b=row(ssm_conv_b), dt_bias=pad_groups(ssm_dt_bias),
                 a_log=pad_groups(ssm_a_log), d_skip=row(jnp.repeat(ssm_d, SSM_HEADDIM)),
                 norm_g=row(ssm_norm_g)),
        merge=(gla_wo.astype(BF16), rwkv_wo.astype(BF16), ssm_wo.astype(BF16), w_out.astype(BF16),
               row(ln1_g), row(ln1_b)),
        moe=(jnp.transpose(w_router).astype(BF16), router_bias.reshape(-1, 1),
             exp_w1.astype(BF16), exp_w3.astype(BF16), exp_w2.astype(BF16),
             sh_w1.astype(BF16), sh_w3.astype(BF16), sh_w2.astype(BF16), row(ln2_g), row(ln2_b)),
    )


def _layer(h, st_gla, st_rwkv, st_shift, st_ssm, st_conv, lp, b, l):
    u = _inproj_call(h, lp["w_u"])
    o_gla, s_gla = _gla_call(u, st_gla, lp["gla"]["wa2"], lp["gla"]["ba"], lp["gla"]["ng"], b, l)
    o_rwkv, s_rwkv = _rwkv_call(u, st_shift, st_rwkv, lp["rwkv"], b, l)
    o_ssm, s_ssm = _ssd_call(u, st_conv, st_ssm, lp["ssm"], b, l)
    h1 = _merge_call(u, o_gla, o_rwkv, o_ssm, h, *lp["merge"])
    h2 = _moe_call(h1, *lp["moe"])
    u3 = u.reshape(b, l, N_U)
    shift_new = jnp.concatenate([u3[:, l - 1:, OFF_RWKV:OFF_RWKV + 3072],
                                 u3[:, l - 1:, OFF_RLOW:OFF_RLOW + 256]], axis=-1)
    conv_new = u3[:, l - (SSM_CONV - 1):, OFF_SSM + SSM_INNER:OFF_SSM + SSM_INNER + SSM_CONV_DIM]
    return h2, s_gla, s_rwkv, shift_new, s_ssm, conv_new


def _zero_states(batch):
    return (jnp.zeros((batch, GLA_HEADS, GLA_DK, GLA_DV), F32),
            jnp.zeros((batch, RWKV_HEADS, RWKV_HEAD, RWKV_HEAD), F32),
            jnp.zeros((batch, 1, RWKV_PROJ), F32),
            jnp.zeros((batch, SSM_HEADS, SSM_HEADDIM, SSM_STATE), F32),
            jnp.zeros((batch, SSM_CONV - 1, SSM_CONV_DIM), F32))


def kernel(x_prompt, x_sample, state_gla, state_rwkv, state_rwkv_shift, state_ssm, state_ssm_conv,
           ln_in_g, ln_in_b, w_in, gla_wa2, gla_ba, gla_norm_g, gla_wo,
           rwkv_mu, rwkv_w0, rwkv_w2, rwkv_a0, rwkv_a2, rwkv_g2, rwkv_k_k, rwkv_k_a, rwkv_r_k,
           rwkv_ln_g, rwkv_ln_b, rwkv_wo,
           ssm_conv_w, ssm_conv_b, ssm_dt_bias, ssm_a_log, ssm_d, ssm_norm_g, ssm_wo,
           w_out, ln1_g, ln1_b,
           w_router, router_bias, exp_w1, exp_w3, exp_w2, sh_w1, sh_w3, sh_w2, ln2_g, ln2_b):
    layer_params = (w_in, gla_wa2, gla_ba, gla_norm_g, gla_wo,
                    rwkv_mu, rwkv_w0, rwkv_w2, rwkv_a0, rwkv_a2, rwkv_g2, rwkv_k_k, rwkv_k_a, rwkv_r_k,
                    rwkv_ln_g, rwkv_ln_b, rwkv_wo,
                    ssm_conv_w, ssm_conv_b, ssm_dt_bias, ssm_a_log, ssm_d, ssm_norm_g, ssm_wo,
                    w_out, ln1_g, ln1_b,
                    w_router, router_bias, exp_w1, exp_w3, exp_w2, sh_w1, sh_w3, sh_w2, ln2_g, ln2_b)
    depth = w_in.shape[0]
    bp, lp_, d = x_prompt.shape
    bs, ls, _ = x_sample.shape
    h_p = _ln_call(x_prompt.reshape(bp * lp_, d), ln_in_g, ln_in_b)
    h_s = _ln_call(x_sample.reshape(bs * ls, d), ln_in_g, ln_in_b)
    new_p, new_s = [], []
    for i in range(depth):
        lp = _prep_layer(*[t[i] for t in layer_params])
        h_p, *st_p = _layer(h_p, *_zero_states(bp), lp, bp, lp_)
        new_p.append(st_p)
        h_s, *st_s = _layer(h_s, state_gla[i], state_rwkv[i], state_rwkv_shift[i], state_ssm[i],
                            state_ssm_conv[i], lp, bs, ls)
        new_s.append(st_s)
    stack = lambda rows: tuple(jnp.stack(z) for z in zip(*rows))
    return (h_p.reshape(bp, lp_, d), h_s.reshape(bs, ls, d)) + stack(new_p) + stack(new_s)
```

```python
import functools

import jax
import jax.numpy as jnp
from jax import lax
from jax.experimental import pallas as pl
from jax.experimental.pallas import tpu as pltpu

F32 = jnp.float32
BF16 = jnp.bfloat16

D_MODEL = 1024
CHUNK = 64
GLA_HEADS, GLA_DK, GLA_DV, GLA_RANK = 4, 128, 256, 16
GLA_GATE_NORM = 16.0
GLA_NORM_EPS = 1e-5
RWKV_HEAD, RWKV_HEADS, RWKV_DIM = 64, 16, 1024
RWKV_PROJ = 3 * RWKV_DIM + 64 + 64 + 128
RWKV_GN_EPS = 64e-5
SSM_INNER, SSM_HEADDIM, SSM_HEADS, SSM_GROUPS, SSM_STATE, SSM_CONV = 2048, 64, 32, 4, 128, 4
SSM_CONV_DIM = SSM_INNER + 2 * SSM_GROUPS * SSM_STATE
SSM_NORM_EPS = 1e-5
N_EXPERTS, TOP_K, N_EXPERT_GROUPS, TOPK_GROUPS, D_EXPERT = 64, 8, 8, 4, 256
ROUTED_SCALE = 2.5
DEPTH = 2
DN_ALPHA = (2.0 * DEPTH) ** 0.25
LN_EPS = 1e-5

OFF_GATE, OFF_GLA, OFF_RWKV, OFF_SSM = 0, 3072, 6144, 9216
OFF_RLOW, OFF_GALOW, OFF_DT = 14336, 14592, 14720
N_U = 15360
VMEM_LIMIT = 56 * 1024 * 1024


def _mm(a, b):
    return jnp.dot(a.astype(BF16), b.astype(BF16), preferred_element_type=F32)


def _mm_nt(a, b):
    return lax.dot_general(a.astype(BF16), b.astype(BF16), (((1,), (1,)), ((), ())),
                           preferred_element_type=F32)


def _mm_tn(a, b):
    return lax.dot_general(a.astype(BF16), b.astype(BF16), (((0,), (0,)), ((), ())),
                           preferred_element_type=F32)


def _sigmoid(x):
    return 1.0 / (1.0 + jnp.exp(-x))


def _silu(x):
    return x * _sigmoid(x)


def _softplus(x):
    return jnp.maximum(x, 0.0) + jnp.log(1.0 + jnp.exp(-jnp.abs(x)))


def _layer_norm(x, g, b):
    mu = jnp.mean(x, -1, keepdims=True)
    xc = x - mu
    var = jnp.mean(xc * xc, -1, keepdims=True)
    return xc * lax.rsqrt(var + LN_EPS) * g + b


def _tril_mask(n, strict=False):
    r = lax.broadcasted_iota(jnp.int32, (n, n), 0)
    c = lax.broadcasted_iota(jnp.int32, (n, n), 1)
    return (r > c) if strict else (r >= c)


def _cumsum_rows(x, tril_bf16):
    x1 = x.astype(BF16)
    r1 = x - x1.astype(F32)
    x2 = r1.astype(BF16)
    x3 = (r1 - x2.astype(F32)).astype(BF16)
    dot = lambda p: jnp.dot(tril_bf16, p, preferred_element_type=F32)
    return dot(x1) + dot(x2) + dot(x3)


def _ln_kernel(x_ref, g_ref, b_ref, o_ref):
    o_ref[...] = _layer_norm(x_ref[...], g_ref[...], b_ref[...])


def _ln_call(x, g, b):
    t = x.shape[0]
    tm = min(t, 512)
    return pl.pallas_call(
        _ln_kernel,
        name="ln_in",
        grid=(t // tm,),
        in_specs=[pl.BlockSpec((tm, D_MODEL), lambda i: (i, 0)),
                  pl.BlockSpec((1, D_MODEL), lambda i: (0, 0)),
                  pl.BlockSpec((1, D_MODEL), lambda i: (0, 0))],
        out_specs=pl.BlockSpec((tm, D_MODEL), lambda i: (i, 0)),
        out_shape=jax.ShapeDtypeStruct((t, D_MODEL), F32),
        compiler_params=pltpu.CompilerParams(dimension_semantics=("parallel",)),
    )(x, g.reshape(1, -1), b.reshape(1, -1))


def _inproj_kernel(x_ref, w_ref, o_ref, xb_ref):
    @pl.when(pl.program_id(1) == 0)
    def _():
        xb_ref[...] = x_ref[...].astype(BF16)

    o_ref[...] = jnp.dot(xb_ref[...], w_ref[...], preferred_element_type=F32)


def _inproj_call(h, w):
    t, n = h.shape[0], w.shape[1]
    tm = min(t, 2048)
    tn = 1024
    return pl.pallas_call(
        _inproj_kernel,
        name="inproj",
        grid=(t // tm, n // tn),
        in_specs=[pl.BlockSpec((tm, D_MODEL), lambda i, j: (i, 0)),
                  pl.BlockSpec((D_MODEL, tn), lambda i, j: (0, j))],
        out_specs=pl.BlockSpec((tm, tn), lambda i, j: (i, j)),
        out_shape=jax.ShapeDtypeStruct((t, n), F32),
        scratch_shapes=[pltpu.VMEM((tm, D_MODEL), BF16)],
        compiler_params=pltpu.CompilerParams(dimension_semantics=("parallel", "arbitrary"),
                                             vmem_limit_bytes=VMEM_LIMIT),
    )(h, w)


def _gla_kernel(q_ref, k_ref, v_ref, r_ref, al_ref, wa2_ref, ba_ref, ng_ref, s0_ref,
                o_ref, sn_ref, s_ref, *, nc):
    i = pl.program_id(2)

    @pl.when(i == 0)
    def _():
        s_ref[...] = s0_ref[0, 0]

    tril = _tril_mask(CHUNK)
    tril_b = tril.astype(BF16)
    eye = (lax.broadcasted_iota(jnp.int32, (GLA_DK, GLA_DK), 0)
           == lax.broadcasted_iota(jnp.int32, (GLA_DK, GLA_DK), 1))
    chunks = range(nc)
    sl = [pl.ds(c * CHUNK, CHUNK) for c in chunks]
    z = _mm(al_ref[...], wa2_ref[...]) + ba_ref[...]
    la = -_softplus(-z) * (1.0 / GLA_GATE_NORM)
    g = [_cumsum_rows(la[c * CHUNK:(c + 1) * CHUNK], tril_b) for c in chunks]
    g_last = [g[c][CHUNK - 1:CHUNK, :] for c in chunks]
    q_dec = [q_ref[sl[c], :] * jnp.exp(g[c]) * (GLA_DK ** -0.5) for c in chunks]
    k_inv = [k_ref[sl[c], :] * jnp.exp(-g[c]) for c in chunks]
    k_end = [k_ref[sl[c], :] * jnp.exp(g_last[c] - g[c]) for c in chunks]
    scores = [jnp.where(tril, _mm_nt(q_dec[c], k_inv[c]), 0.0) for c in chunks]
    kv = [_mm_tn(k_end[c], v_ref[sl[c], :]) for c in chunks]
    o_intra = [_mm(scores[c], v_ref[sl[c], :]) for c in chunks]
    dec_col = [jnp.sum(jnp.where(eye, jnp.exp(g_last[c]), 0.0), axis=1, keepdims=True) for c in chunks]
    s_cur = s_ref[...]
    for c in chunks:
        o = o_intra[c] + _mm(q_dec[c], s_cur)
        s_cur = s_cur * dec_col[c] + kv[c]
        o = o * lax.rsqrt(jnp.mean(o * o, -1, keepdims=True) + GLA_NORM_EPS) * ng_ref[...]
        o_ref[sl[c], :] = (o * _silu(r_ref[sl[c], :])).astype(o_ref.dtype)
    s_ref[...] = s_cur

    @pl.when(i == pl.num_programs(2) - 1)
    def _():
        sn_ref[0, 0] = s_ref[...]


def _gla_call(u, s0, wa2p, ba, ng, b, l):
    lb = min(l, 1024)
    nb = l // lb
    row = lambda bb, h, i: bb * nb + i
    spec = lambda w, off: pl.BlockSpec((lb, w), lambda bb, h, i: (row(bb, h, i), off + h))
    return pl.pallas_call(
        functools.partial(_gla_kernel, nc=lb // CHUNK),
        name="gla_mixer",
        grid=(b, GLA_HEADS, nb),
        in_specs=[spec(128, OFF_GLA // 128), spec(128, (OFF_GLA + 512) // 128),
                  spec(256, (OFF_GLA + 1024) // 256), spec(256, (OFF_GLA + 2048) // 256),
                  pl.BlockSpec((lb, 128), lambda bb, h, i: (row(bb, h, i), OFF_GALOW // 128)),
                  pl.BlockSpec((128, 128), lambda bb, h, i: (0, h)),
                  pl.BlockSpec((1, 128), lambda bb, h, i: (0, h)),
                  pl.BlockSpec((1, 256), lambda bb, h, i: (0, 0)),
                  pl.BlockSpec((1, 1, GLA_DK, GLA_DV), lambda bb, h, i: (bb, h, 0, 0))],
        out_specs=[pl.BlockSpec((lb, 256), lambda bb, h, i: (row(bb, h, i), h)),
                   pl.BlockSpec((1, 1, GLA_DK, GLA_DV), lambda bb, h, i: (bb, h, 0, 0))],
        out_shape=[jax.ShapeDtypeStruct((b * l, GLA_HEADS * GLA_DV), BF16),
                   jax.ShapeDtypeStruct((b, GLA_HEADS, GLA_DK, GLA_DV), F32)],
        scratch_shapes=[pltpu.VMEM((GLA_DK, GLA_DV), F32)],
        compiler_params=pltpu.CompilerParams(dimension_semantics=("parallel", "parallel", "arbitrary")),
    )(u, u, u, u, u, wa2p, ba, ng, s0)


def _half_sum(x, lo):
    s0 = jnp.sum(jnp.where(lo, x, 0.0), -1, keepdims=True)
    s1 = jnp.sum(jnp.where(lo, 0.0, x), -1, keepdims=True)
    return jnp.where(lo, s0, s1)


def _rwkv_kernel(r_ref, k_ref, v_ref, low_ref, shr_ref, shk_ref, shv_ref, shl_ref,
                 mur_ref, muk_ref, muv_ref, mul_ref, w0_ref, w2_ref, a0_ref, a2_ref, g2_ref,
                 kk_ref, ka_ref, rk_ref, lng_ref, lnb_ref, s0_ref,
                 o_ref, sn_ref, s_ref, sh_ref, *, nc):
    i = pl.program_id(2)
    lb = nc * CHUNK
    hd = RWKV_HEAD

    @pl.when(i == 0)
    def _():
        zero = jnp.zeros((hd, hd), F32)
        s_ref[...] = jnp.concatenate([jnp.concatenate([s0_ref[0, 0], zero], axis=1),
                                      jnp.concatenate([zero, s0_ref[0, 1]], axis=1)], axis=0)
        sh_ref[7:8, 0:128] = shr_ref[0]
        sh_ref[7:8, 128:256] = shk_ref[0]
        sh_ref[7:8, 256:384] = shv_ref[0]
        sh_ref[7:8, 384:640] = shl_ref[0]

    sh_ref[8:8 + lb, 0:128] = r_ref[...]
    sh_ref[8:8 + lb, 128:256] = k_ref[...]
    sh_ref[8:8 + lb, 256:384] = v_ref[...]
    sh_ref[8:8 + lb, 384:640] = low_ref[...]

    tril_b = _tril_mask(CHUNK).astype(BF16)
    row = lax.broadcasted_iota(jnp.int32, (CHUNK, 128), 0)
    pos = jnp.bitwise_and(lax.broadcasted_iota(jnp.int32, (CHUNK, 128), 1), jnp.int32(hd - 1))
    tril, stril, eye_f = row >= pos, row > pos, (row == pos).astype(F32)
    lo = lax.broadcasted_iota(jnp.int32, (1, 128), 1) < hd

    prep = []
    for c in range(nc):
        cur = sh_ref[8 + c * CHUNK:8 + (c + 1) * CHUNK, :]
        prev = sh_ref[7 + c * CHUNK:7 + (c + 1) * CHUNK, :]
        shift = lambda a, b, mu: a + (b - a) * mu
        r = shift(cur[:, 0:128], prev[:, 0:128], mur_ref[...])
        k = shift(cur[:, 128:256], prev[:, 128:256], muk_ref[...])
        v = shift(cur[:, 256:384], prev[:, 256:384], muv_ref[...])
        low = shift(cur[:, 384:640], prev[:, 384:640], mul_ref[...])
        zw = w0_ref[...] + _mm(jnp.tanh(low[:, 0:64]), w2_ref[...])
        lw = -jnp.exp(-_softplus(-zw) - 0.5)
        a = _sigmoid(a0_ref[...] + _mm(low[:, 64:128], a2_ref[...]))
        gate = _mm(_sigmoid(low[:, 128:256]), g2_ref[...])
        kk = k * kk_ref[...]
        kk = kk * lax.rsqrt(jnp.maximum(_half_sum(kk * kk, lo), 1e-24))
        k2 = k * (1.0 + (a - 1.0) * ka_ref[...])
        lg = _cumsum_rows(lw, tril_b)
        e_neg = jnp.exp(-lg)
        prep.append(dict(a=-kk * jnp.exp(lg - lw), b=kk * a * e_neg, k=k2 * e_neg, r=r * jnp.exp(lg),
                         v=v, gam=jnp.exp(lg[CHUNK - 1:CHUNK, :]), gate=gate,
                         bonus=_half_sum(r * k2 * rk_ref[...], lo) * v))

    chunks = range(nc)
    bzero = jnp.zeros((), BF16)

    def bd(x):
        xb = x.astype(BF16)
        return jnp.concatenate([jnp.where(lo, xb, bzero), jnp.where(lo, bzero, xb)], axis=0)

    cast = lambda x: x.astype(BF16)
    dot = lambda a, b: jnp.dot(a, b, preferred_element_type=F32)
    dot_nt = lambda a, b: lax.dot_general(a, b, (((1,), (1,)), ((), ())), preferred_element_type=F32)
    dot_tn = lambda a, b: lax.dot_general(a, b, (((0,), (0,)), ((), ())), preferred_element_type=F32)
    get = lambda name: [prep[c][name] for c in chunks]
    at, bt, kt, rt, vv, gam = get("a"), get("b"), get("k"), get("r"), get("v"), get("gam")
    bd_b, bd_k, bd_v, bd_a = ([bd(x[c]) for c in chunks] for x in (bt, kt, vv, at))
    ar = [cast(jnp.concatenate([at[c], rt[c]], axis=0)) for c in chunks]
    pb = [dot_nt(ar[c], bd_b[c]) for c in chunks]
    pk = [dot_nt(ar[c], bd_k[c]) for c in chunks]
    n_ab = [jnp.where(stril, pb[c][:CHUNK], 0.0) for c in chunks]
    m_rb = [jnp.where(tril, pb[c][CHUNK:], 0.0) for c in chunks]
    n_ak = [jnp.where(stril, pk[c][:CHUNK], 0.0) for c in chunks]
    m_rk = [jnp.where(tril, pk[c][CHUNK:], 0.0) for c in chunks]
    t_inv = [eye_f + n_ab[c] for c in chunks]
    pw = n_ab
    bd_pw = [bd(pw[c]) for c in chunks]
    for _ in range(5):
        pw = [dot(cast(pw[c]), bd_pw[c]) for c in chunks]
        bd_pw = [bd(pw[c]) for c in chunks]
        t_inv = [t_inv[c] + dot(cast(t_inv[c]), bd_pw[c]) for c in chunks]
    akv = [dot(cast(n_ak[c]), bd_v[c]) for c in chunks]
    au = [dot(cast(t_inv[c]), jnp.concatenate([bd_a[c], bd(akv[c])], axis=1)) for c in chunks]
    a_hat = [au[c][:, 0:128] for c in chunks]
    u0 = [au[c][:, 128:256] for c in chunks]
    bd_ah = [bd(a_hat[c]) for c in chunks]
    bd_u0v = [jnp.concatenate([bd(u0[c]), bd_v[c]], axis=0) for c in chunks]
    g_lr = [dot_tn(bd_ah[c], bd_b[c]) * gam[c] for c in chunks]
    c0 = [dot_tn(bd_u0v[c], jnp.concatenate([bd_b[c], bd_k[c]], axis=0)) * gam[c] for c in chunks]
    r_hat = [rt[c] + dot(cast(m_rb[c]), bd_ah[c]) for c in chunks]
    y0 = [dot(cast(jnp.concatenate([m_rb[c], m_rk[c]], axis=1)), bd_u0v[c]) for c in chunks]

    s_cur = s_ref[...]
    for c in chunks:
        s_b = cast(s_cur)
        y = dot_nt(cast(r_hat[c]), s_b) + y0[c]
        s_cur = s_cur * gam[c] + dot(s_b, cast(g_lr[c])) + c0[c]
        mean = _half_sum(y, lo) * (1.0 / hd)
        yc = y - mean
        var = _half_sum(yc * yc, lo) * (1.0 / hd)
        y = yc * lax.rsqrt(var + RWKV_GN_EPS) * lng_ref[...] + lnb_ref[...]
        o_ref[c * CHUNK:(c + 1) * CHUNK, :] = ((y + prep[c]["bonus"]) * prep[c]["gate"]).astype(o_ref.dtype)
    s_ref[...] = s_cur

    sh_ref[7:8, :] = sh_ref[7 + lb:8 + lb, :]

    @pl.when(i == pl.num_programs(2) - 1)
    def _():
        sn_ref[0, 0] = s_ref[0:hd, 0:hd]
        sn_ref[0, 1] = s_ref[hd:2 * hd, hd:2 * hd]


def _rwkv_call(u, shift_buf, s0, p, b, l):
    lb = min(l, 2048)
    nb = l // lb
    row = lambda bb, h, i: bb * nb + i
    ublk = lambda off: pl.BlockSpec((lb, 128), lambda bb, h, i: (row(bb, h, i), off // 128 + h))
    sblk = lambda off: pl.BlockSpec((1, 1, 128), lambda bb, h, i: (bb, 0, off // 128 + h))
    vec = lambda: pl.BlockSpec((1, 128), lambda bb, h, i: (0, h))
    const = lambda shape: pl.BlockSpec(shape, lambda bb, h, i: (0,) * len(shape))
    return pl.pallas_call(
        functools.partial(_rwkv_kernel, nc=lb // CHUNK),
        name="rwkv_mixer",
        grid=(b, RWKV_HEADS // 2, nb),
        in_specs=[ublk(OFF_RWKV), ublk(OFF_RWKV + 1024), ublk(OFF_RWKV + 2048),
                  pl.BlockSpec((lb, 256), lambda bb, h, i: (row(bb, h, i), OFF_RLOW // 256)),
                  sblk(0), sblk(1024), sblk(2048),
                  pl.BlockSpec((1, 1, 256), lambda bb, h, i: (bb, 0, 3072 // 256)),
                  vec(), pl.BlockSpec((1, 128), lambda bb, h, i: (0, 8 + h)),
                  pl.BlockSpec((1, 128), lambda bb, h, i: (0, 16 + h)),
                  pl.BlockSpec((1, 256), lambda bb, h, i: (0, 3072 // 256)),
                  vec(), pl.BlockSpec((64, 128), lambda bb, h, i: (0, h)),
                  vec(), pl.BlockSpec((64, 128), lambda bb, h, i: (0, h)),
                  pl.BlockSpec((128, 128), lambda bb, h, i: (0, h)),
                  vec(), vec(), vec(), vec(), vec(),
                  pl.BlockSpec((1, 2, RWKV_HEAD, RWKV_HEAD), lambda bb, h, i: (bb, h, 0, 0))],
        out_specs=[pl.BlockSpec((lb, 128), lambda bb, h, i: (row(bb, h, i), h)),
                   pl.BlockSpec((1, 2, RWKV_HEAD, RWKV_HEAD), lambda bb, h, i: (bb, h, 0, 0))],
        out_shape=[jax.ShapeDtypeStruct((b * l, RWKV_DIM), BF16),
                   jax.ShapeDtypeStruct((b, RWKV_HEADS, RWKV_HEAD, RWKV_HEAD), F32)],
        scratch_shapes=[pltpu.VMEM((2 * RWKV_HEAD, 2 * RWKV_HEAD), F32),
                        pltpu.VMEM((lb + 8, 640), F32)],
        compiler_params=pltpu.CompilerParams(dimension_semantics=("parallel", "parallel", "arbitrary")),
    )(u, u, u, u, shift_buf, shift_buf, shift_buf, shift_buf,
      p["mu"], p["mu"], p["mu"], p["mu"], p["w0"], p["w2"], p["a0"], p["a2"], p["g2"],
      p["k_k"], p["k_a"], p["r_k"], p["ln_g"], p["ln_b"], s0)


def _ssd_kernel(z_ref, x_ref, bm_ref, cm_ref, dt_ref, cbx_ref, cbb_ref, cbc_ref,
                cwx_ref, cwb_ref, cwc_ref, cbiasx_ref, cbiasb_ref, cbiasc_ref,
                dtb_ref, alog_ref, dsk_ref, ng_ref, s0_ref,
                o_ref, sn_ref, s_ref, full_ref, *, nc):
    i = pl.program_id(2)
    lb = nc * CHUNK
    hpg = SSM_HEADS // SSM_GROUPS
    p = SSM_HEADDIM
    wx = hpg * p

    @pl.when(i == 0)
    def _():
        s_ref[...] = s0_ref[0].reshape(hpg * p, SSM_STATE)
        full_ref[5:8, 0:wx] = cbx_ref[0]
        full_ref[5:8, wx:wx + 128] = cbb_ref[0]
        full_ref[5:8, wx + 128:wx + 256] = cbc_ref[0]

    rnd = lambda a: a.astype(BF16).astype(F32)

    @pl.when(i == 0)
    def _():
        full_ref[5:8, :] = rnd(full_ref[5:8, :])
        full_ref[8 + lb:16 + lb, :] = jnp.zeros((8, wx + 256), F32)
        full_ref[0:5, :] = jnp.zeros((5, wx + 256), F32)

    full_ref[8:8 + lb, 0:wx] = rnd(x_ref[...])
    full_ref[8:8 + lb, wx:wx + 128] = rnd(bm_ref[...])
    full_ref[8:8 + lb, wx + 128:wx + 256] = rnd(cm_ref[...])

    tril = _tril_mask(CHUNK)
    tril_b = tril.astype(BF16)
    cw = jnp.concatenate([cwx_ref[...], cwb_ref[...], cwc_ref[...]], axis=1)
    cbias = jnp.concatenate([cbiasx_ref[...], cbiasb_ref[...], cbiasc_ref[...]], axis=1)
    a_row = -jnp.exp(alog_ref[0])

    iota = lambda shape, d: lax.broadcasted_iota(jnp.int32, shape, d)
    head_of = lambda idx: lax.shift_right_logical(idx, jnp.int32(6))
    expand = (iota((128, wx), 0) == head_of(iota((128, wx), 1))).astype(BF16)
    pos = jnp.bitwise_and(iota((CHUNK, wx), 1), jnp.int32(CHUNK - 1))
    tril_t = iota((CHUNK, wx), 0) >= pos
    diag_t = iota((CHUNK, wx), 0) == pos
    blockdiag = head_of(iota((wx, wx), 0)) == head_of(iota((wx, wx), 1))

    def split3(x):
        x1 = x.astype(BF16)
        r1 = x - x1.astype(F32)
        x2 = r1.astype(BF16)
        return x1, x2, (r1 - x2.astype(F32)).astype(BF16)

    dot = lambda a, b: jnp.dot(a, b, preferred_element_type=F32)
    per_head = lambda x: sum(dot(piece, expand) for piece in split3(x))

    chunks = range(nc)
    xs, bm, cm, dtv, acum = [], [], [], [], []
    win = CHUNK + 16
    shifts = [(iota((CHUNK, win), 1) == iota((CHUNK, win), 0) + jnp.int32(5 + j)).astype(BF16)
              for j in range(SSM_CONV - 1)]
    for c in chunks:
        base = c * CHUNK
        window = full_ref[base:base + win, :].astype(BF16)
        conv = cbias + cw[SSM_CONV - 1:SSM_CONV, :] * full_ref[8 + base:8 + base + CHUNK, :]
        for j in range(SSM_CONV - 1):
            conv = conv + cw[j:j + 1, :] * dot(shifts[j], window)
        conv = _silu(conv)
        xs.append(conv[:, 0:wx])
        bm.append(conv[:, wx:wx + 128])
        cm.append(conv[:, wx + 128:wx + 256])
        dtv.append(_softplus(dt_ref[base:base + CHUNK, :] + dtb_ref[0]))
    acum = [_cumsum_rows(dtv[c] * a_row, tril_b) for c in chunks]
    a_col = [per_head(acum[c]) for c in chunks]
    a_row_e = [jnp.sum(jnp.where(diag_t, a_col[c], 0.0), axis=0, keepdims=True)
               for c in chunks]
    dt_e = [per_head(dtv[c]) for c in chunks]
    cb_t = [_mm_nt(cm[c], jnp.concatenate([bm[c]] * hpg, axis=0)) for c in chunks]
    dec = [jnp.where(tril_t, jnp.exp(jnp.where(tril_t, a_col[c] - a_row_e[c], 0.0)), 0.0) for c in chunks]
    xdt = [xs[c] * dt_e[c] for c in chunks]
    xdt_bd = [jnp.where(blockdiag, jnp.concatenate([xdt[c].astype(BF16)] * hpg, axis=0),
                        jnp.zeros((), BF16)) for c in chunks]
    y_diag = [dot((cb_t[c] * dec[c]).astype(BF16), xdt_bd[c]) for c in chunks]
    a_last = [a_col[c][CHUNK - 1:CHUNK, :] for c in chunks]
    st = [_mm_tn(xdt[c] * jnp.exp(a_last[c] - a_col[c]), bm[c]) for c in chunks]
    e_end = [jnp.exp(acum[c][CHUNK - 1:CHUNK, :]) for c in chunks]
    e_in = [jnp.exp(a_col[c]) for c in chunks]
    s_cur = s_ref[...]
    for c in chunks:
        base = c * CHUNK
        y = y_diag[c] + _mm_nt(cm[c], s_cur) * e_in[c] + xs[c] * dsk_ref[...]
        s_cur = jnp.concatenate([s_cur[h * p:(h + 1) * p, :] * e_end[c][:, h:h + 1] for h in range(hpg)],
                                axis=0) + st[c]
        y = y * _silu(z_ref[base:base + CHUNK, :])
        y = y * lax.rsqrt(jnp.mean(y * y, -1, keepdims=True) + SSM_NORM_EPS) * ng_ref[...]
        o_ref[base:base + CHUNK, :] = y.astype(o_ref.dtype)
    s_ref[...] = s_cur

    full_ref[5:8, :] = full_ref[5 + lb:8 + lb, :]

    @pl.when(i == pl.num_programs(2) - 1)
    def _():
        sn_ref[0] = s_ref[...].reshape(hpg, p, SSM_STATE)


def _ssd_call(u, conv_buf, s0, p, b, l):
    lb = min(l, 256)
    nb = l // lb
    hpg = SSM_HEADS // SSM_GROUPS
    row = lambda bb, g, i: bb * nb + i
    ublk = lambda w, off: pl.BlockSpec((lb, w), lambda bb, g, i: (row(bb, g, i), off // w + g))
    gvec = lambda w, off: pl.BlockSpec((1, w), lambda bb, g, i: (0, off // w + g))
    gtap = lambda w, off: pl.BlockSpec((SSM_CONV, w), lambda bb, g, i: (0, off // w + g))
    gbuf = lambda w, off: pl.BlockSpec((1, SSM_CONV - 1, w), lambda bb, g, i: (bb, 0, off // w + g))
    return pl.pallas_call(
        functools.partial(_ssd_kernel, nc=lb // CHUNK),
        name="ssd_mixer",
        grid=(b, SSM_GROUPS, nb),
        in_specs=[ublk(512, OFF_SSM), ublk(512, OFF_SSM + 2048),
                  ublk(128, OFF_SSM + 4096), ublk(128, OFF_SSM + 4608), ublk(128, OFF_DT),
                  gbuf(512, 0), gbuf(128, 2048), gbuf(128, 2560),
                  gtap(512, 0), gtap(128, 2048), gtap(128, 2560),
                  gvec(512, 0), gvec(128, 2048), gvec(128, 2560),
                  pl.BlockSpec((1, 1, 128), lambda bb, g, i: (g, 0, 0)),
                  pl.BlockSpec((1, 1, 128), lambda bb, g, i: (g, 0, 0)),
                  gvec(512, 0), gvec(512, 0),
                  pl.BlockSpec((1, hpg, SSM_HEADDIM, SSM_STATE), lambda bb, g, i: (bb, g, 0, 0))],
        out_specs=[pl.BlockSpec((lb, 512), lambda bb, g, i: (row(bb, g, i), g)),
                   pl.BlockSpec((1, hpg, SSM_HEADDIM, SSM_STATE), lambda bb, g, i: (bb, g, 0, 0))],
        out_shape=[jax.ShapeDtypeStruct((b * l, SSM_INNER), BF16),
                   jax.ShapeDtypeStruct((b, SSM_HEADS, SSM_HEADDIM, SSM_STATE), F32)],
        scratch_shapes=[pltpu.VMEM((hpg * SSM_HEADDIM, SSM_STATE), F32),
                        pltpu.VMEM((lb + 16, 768), F32)],
        compiler_params=pltpu.CompilerParams(dimension_semantics=("parallel", "parallel", "arbitrary")),
    )(u, u, u, u, u, conv_buf, conv_buf, conv_buf,
      p["conv_w"], p["conv_w"], p["conv_w"], p["conv_b"], p["conv_b"], p["conv_b"],
      p["dt_bias"], p["a_log"], p["d_skip"], p["norm_g"], s0)


def _merge_kernel(gate_ref, og_ref, or_ref, os_ref, x_ref, wg_ref, wr_ref, ws_ref, wo_ref,
                  g_ref, b_ref, o_ref):
    d = D_MODEL
    m = _sigmoid(gate_ref[:, 0:d]) * jnp.dot(og_ref[...], wg_ref[...], preferred_element_type=F32)
    m = m + _sigmoid(gate_ref[:, d:2 * d]) * jnp.dot(or_ref[...], wr_ref[...], preferred_element_type=F32)
    m = m + _sigmoid(gate_ref[:, 2 * d:3 * d]) * jnp.dot(os_ref[...], ws_ref[...], preferred_element_type=F32)
    y = DN_ALPHA * x_ref[...] + _mm(m, wo_ref[...])
    o_ref[...] = _layer_norm(y, g_ref[...], b_ref[...])


def _merge_call(u, og, orr, os_, x, wg, wr, ws, wo, g, b):
    t = x.shape[0]
    tm = min(t, 256)
    tok = lambda w: pl.BlockSpec((tm, w), lambda i: (i, 0))
    const = lambda r, c: pl.BlockSpec((r, c), lambda i: (0, 0))
    return pl.pallas_call(
        _merge_kernel,
        name="merge_outproj_ln",
        grid=(t // tm,),
        in_specs=[tok(3 * D_MODEL), tok(D_MODEL), tok(D_MODEL), tok(SSM_INNER), tok(D_MODEL),
                  const(D_MODEL, D_MODEL), const(D_MODEL, D_MODEL), const(SSM_INNER, D_MODEL),
                  const(D_MODEL, D_MODEL), const(1, D_MODEL), const(1, D_MODEL)],
        out_specs=tok(D_MODEL),
        out_shape=jax.ShapeDtypeStruct((t, D_MODEL), F32),
        compiler_params=pltpu.CompilerParams(dimension_semantics=("parallel",),
                                             vmem_limit_bytes=VMEM_LIMIT),
    )(u, og, orr, os_, x, wg, wr, ws, wo, g, b)


def _router_gates(xb, wrt_ref, rb_ref, key_ref):
    tm = xb.shape[0]
    per_group = N_EXPERTS // N_EXPERT_GROUPS
    s_t = _sigmoid(_mm_nt(wrt_ref[...], xb))
    sb = s_t + rb_ref[...]
    gscore = []
    for g in range(N_EXPERT_GROUPS):
        xg = sb[g * per_group:(g + 1) * per_group, :]
        m1 = jnp.max(xg, axis=0, keepdims=True)
        eq = xg == m1
        cnt = jnp.sum(eq.astype(F32), axis=0, keepdims=True)
        m2 = jnp.max(jnp.where(eq, -jnp.inf, xg), axis=0, keepdims=True)
        gscore.append(m1 + jnp.where(cnt >= 2.0, m1, m2))
    for g in range(N_EXPERT_GROUPS):
        rank = jnp.zeros((1, tm), F32)
        for g2 in range(N_EXPERT_GROUPS):
            if g2 == g:
                continue
            beats = (gscore[g2] >= gscore[g]) if g2 < g else (gscore[g2] > gscore[g])
            rank = rank + beats.astype(F32)
        keep = rank < float(TOPK_GROUPS)
        key_ref[g * per_group:(g + 1) * per_group, :] = jnp.where(
            keep, sb[g * per_group:(g + 1) * per_group, :], -jnp.inf)
    key = key_ref[...]
    eidx = lax.broadcasted_iota(jnp.int32, (N_EXPERTS, tm), 0)

    def body(ep, rank):
        rowv = key_ref[pl.ds(ep, 1), :]
        beats = (rowv > key) | ((rowv == key) & (ep < eidx))
        return rank + beats.astype(F32)

    rank = lax.fori_loop(0, N_EXPERTS, body, jnp.zeros((N_EXPERTS, tm), F32))
    w = jnp.where(rank < float(TOP_K), s_t, 0.0)
    return w / jnp.sum(w, axis=0, keepdims=True) * ROUTED_SCALE


def _moe_kernel(x_ref, wrt_ref, rb_ref, w1_ref, w3_ref, w2_ref, s1_ref, s3_ref, s2_ref,
                g_ref, b_ref, o_ref, xb_ref, gates_ref, key_ref, *, sub):
    e = pl.program_id(1)
    tm = x_ref.shape[0]

    @pl.when(e == 0)
    def _():
        xb_ref[...] = x_ref[...].astype(BF16)
        g_t = _router_gates(xb_ref[...], wrt_ref, rb_ref, key_ref)
        gates_ref[...] = jnp.transpose(jnp.concatenate([g_t, jnp.zeros_like(g_t)], axis=0))
        for s in range(tm // sub):
            rows = pl.ds(s * sub, sub)
            xs = xb_ref[rows, :]
            hid = _silu(jnp.dot(xs, s1_ref[...], preferred_element_type=F32)) * jnp.dot(
                xs, s3_ref[...], preferred_element_type=F32)
            o_ref[rows, :] = _mm(hid, s2_ref[...])

    lane = lax.broadcasted_iota(jnp.int32, (1, 128), 1)
    per_step = w1_ref.shape[0]
    for s in range(tm // sub):
        rows = pl.ds(s * sub, sub)
        xs = xb_ref[rows, :]
        acc = o_ref[rows, :]
        for j in range(per_step):
            gcol = jnp.sum(jnp.where(lane == e * per_step + j, gates_ref[rows, :], 0.0),
                           axis=1, keepdims=True)
            hid = _silu(jnp.dot(xs, w1_ref[j], preferred_element_type=F32)) * jnp.dot(
                xs, w3_ref[j], preferred_element_type=F32)
            acc = acc + _mm(hid * gcol, w2_ref[j])
        o_ref[rows, :] = acc

    @pl.when(e == pl.num_programs(1) - 1)
    def _():
        o_ref[...] = _layer_norm(DN_ALPHA * x_ref[...] + o_ref[...], g_ref[...], b_ref[...])


def _moe_call(h, wrt, rb, w1, w3, w2, s1, s3, s2, g, b):
    t = h.shape[0]
    tm = min(t, 2048)
    sub = min(tm, 512)
    per_step = 2
    const = lambda r, c: pl.BlockSpec((r, c), lambda i, e: (0, 0))
    return pl.pallas_call(
        functools.partial(_moe_kernel, sub=sub),
        name="moe_ln",
        grid=(t // tm, N_EXPERTS // per_step),
        in_specs=[pl.BlockSpec((tm, D_MODEL), lambda i, e: (i, 0)),
                  const(N_EXPERTS, D_MODEL), const(N_EXPERTS, 1),
                  pl.BlockSpec((per_step, D_MODEL, D_EXPERT), lambda i, e: (e, 0, 0)),
                  pl.BlockSpec((per_step, D_MODEL, D_EXPERT), lambda i, e: (e, 0, 0)),
                  pl.BlockSpec((per_step, D_EXPERT, D_MODEL), lambda i, e: (e, 0, 0)),
                  const(D_MODEL, D_EXPERT), const(D_MODEL, D_EXPERT), const(D_EXPERT, D_MODEL),
                  const(1, D_MODEL), const(1, D_MODEL)],
        out_specs=pl.BlockSpec((tm, D_MODEL), lambda i, e: (i, 0)),
        out_shape=jax.ShapeDtypeStruct((t, D_MODEL), F32),
        scratch_shapes=[pltpu.VMEM((tm, D_MODEL), BF16),
                        pltpu.VMEM((tm, 128), F32),
                        pltpu.VMEM((N_EXPERTS, tm), F32)],
        compiler_params=pltpu.CompilerParams(dimension_semantics=("parallel", "arbitrary"),
                                             vmem_limit_bytes=VMEM_LIMIT),
    )(h, wrt, rb, w1, w3, w2, s1, s3, s2, g, b)


def _prep_layer(w_in, gla_wa2, gla_ba, gla_norm_g, gla_wo,
                rwkv_mu, rwkv_w0, rwkv_w2, rwkv_a0, rwkv_a2, rwkv_g2, rwkv_k_k, rwkv_k_a, rwkv_r_k,
                rwkv_ln_g, rwkv_ln_b, rwkv_wo,
                ssm_conv_w, ssm_conv_b, ssm_dt_bias, ssm_a_log, ssm_d, ssm_norm_g, ssm_wo,
                w_out, ln1_g, ln1_b,
                w_router, router_bias, exp_w1, exp_w3, exp_w2, sh_w1, sh_w3, sh_w2, ln2_g, ln2_b):
    d = D_MODEL
    o_gla, o_rwkv = 3 * d, 3 * d + 3088
    o_ssm = o_rwkv + RWKV_PROJ
    zeros = lambda n: jnp.zeros((d, n), w_in.dtype)
    dt0 = o_ssm + SSM_INNER + SSM_CONV_DIM
    hpg = SSM_HEADS // SSM_GROUPS
    dt_cols = []
    for g in range(SSM_GROUPS):
        dt_cols += [w_in[:, dt0 + g * hpg:dt0 + (g + 1) * hpg], zeros(128 - hpg)]
    w_u = jnp.concatenate(
        [w_in[:, 0:3 * d], w_in[:, o_gla:o_gla + 3072], w_in[:, o_rwkv:o_rwkv + 3072],
         w_in[:, o_ssm:o_ssm + SSM_INNER + SSM_CONV_DIM], w_in[:, o_rwkv + 3072:o_rwkv + 3328],
         w_in[:, o_gla + 3072:o_gla + 3088], zeros(128 - GLA_RANK)] + dt_cols + [zeros(128)],
        axis=1).astype(BF16)
    assert w_u.shape[1] == N_U
    row = lambda v: v.reshape(1, -1)
    pad_groups = lambda v: jnp.pad(v.reshape(SSM_GROUPS, 1, hpg), ((0, 0), (0, 0), (0, 128 - hpg)))
    return dict(
        w_u=w_u,
        gla=dict(wa2=jnp.pad(gla_wa2, ((0, 128 - GLA_RANK), (0, 0))).astype(BF16),
                 ba=row(gla_ba), ng=row(gla_norm_g)),
        rwkv=dict(mu=row(rwkv_mu), w0=row(rwkv_w0), w2=rwkv_w2.astype(BF16), a0=row(rwkv_a0),
                  a2=rwkv_a2.astype(BF16), g2=rwkv_g2.astype(BF16), k_k=row(rwkv_k_k),
                  k_a=row(rwkv_k_a), r_k=row(rwkv_r_k), ln_g=row(rwkv_ln_g), ln_b=row(rwkv_ln_b)),
        ssm=dict(conv_w=ssm_conv_w, conv_b=row(ssm_conv_b), dt_bias=pad_groups(ssm_dt_bias),
                 a_log=pad_groups(ssm_a_log), d_skip=row(jnp.repeat(ssm_d, SSM_HEADDIM)),
                 norm_g=row(ssm_norm_g)),
        merge=(gla_wo.astype(BF16), rwkv_wo.astype(BF16), ssm_wo.astype(BF16), w_out.astype(BF16),
               row(ln1_g), row(ln1_b)),
        moe=(jnp.transpose(w_router).astype(BF16), router_bias.reshape(-1, 1),
             exp_w1.astype(BF16), exp_w3.astype(BF16), exp_w2.astype(BF16),
             sh_w1.astype(BF16), sh_w3.astype(BF16), sh_w2.astype(BF16), row(ln2_g), row(ln2_b)),
    )


def _layer(h, st_gla, st_rwkv, st_shift, st_ssm, st_conv, lp, b, l):
    u = _inproj_call(h, lp["w_u"])
    o_gla, s_gla = _gla_call(u, st_gla, lp["gla"]["wa2"], lp["gla"]["ba"], lp["gla"]["ng"], b, l)
    o_rwkv, s_rwkv = _rwkv_call(u, st_shift, st_rwkv, lp["rwkv"], b, l)
    o_ssm, s_ssm = _ssd_call(u, st_conv, st_ssm, lp["ssm"], b, l)
    h1 = _merge_call(u, o_gla, o_rwkv, o_ssm, h, *lp["merge"])
    h2 = _moe_call(h1, *lp["moe"])
    u3 = u.reshape(b, l, N_U)
    shift_new = jnp.concatenate([u3[:, l - 1:, OFF_RWKV:OFF_RWKV + 3072],
                                 u3[:, l - 1:, OFF_RLOW:OFF_RLOW + 256]], axis=-1)
    conv_new = u3[:, l - (SSM_CONV - 1):, OFF_SSM + SSM_INNER:OFF_SSM + SSM_INNER + SSM_CONV_DIM]
    return h2, s_gla, s_rwkv, shift_new, s_ssm, conv_new


def _zero_states(batch):
    return (jnp.zeros((batch, GLA_HEADS, GLA_DK, GLA_DV), F32),
            jnp.zeros((batch, RWKV_HEADS, RWKV_HEAD, RWKV_HEAD), F32),
            jnp.zeros((batch, 1, RWKV_PROJ), F32),
            jnp.zeros((batch, SSM_HEADS, SSM_HEADDIM, SSM_STATE), F32),
            jnp.zeros((batch, SSM_CONV - 1, SSM_CONV_DIM), F32))


def kernel(x_prompt, x_sample, state_gla, state_rwkv, state_rwkv_shift, state_ssm, state_ssm_conv,
           ln_in_g, ln_in_b, w_in, gla_wa2, gla_ba, gla_norm_g, gla_wo,
           rwkv_mu, rwkv_w0, rwkv_w2, rwkv_a0, rwkv_a2, rwkv_g2, rwkv_k_k, rwkv_k_a, rwkv_r_k,
           rwkv_ln_g, rwkv_ln_b, rwkv_wo,
           ssm_conv_w, ssm_conv_b, ssm_dt_bias, ssm_a_log, ssm_d, ssm_norm_g, ssm_wo,
           w_out, ln1_g, ln1_b,
           w_router, router_bias, exp_w1, exp_w3, exp_w2, sh_w1, sh_w3, sh_w2, ln2_g, ln2_b):
    layer_params = (w_in, gla_wa2, gla_ba, gla_norm_g, gla_wo,
                    rwkv_mu, rwkv_w0, rwkv_w2, rwkv_a0, rwkv_a2, rwkv_g2, rwkv_k_k, rwkv_k_a, rwkv_r_k,
                    rwkv_ln_g, rwkv_ln_b, rwkv_wo,
                    ssm_conv_w, ssm_conv_b, ssm_dt_bias, ssm_a_log, ssm_d, ssm_norm_g, ssm_wo,
                    w_out, ln1_g, ln1_b,
                    w_router, router_bias, exp_w1, exp_w3, exp_w2, sh_w1, sh_w3, sh_w2, ln2_g, ln2_b)
    depth = w_in.shape[0]
    bp, lp_, d = x_prompt.shape
    bs, ls, _ = x_sample.shape
    h_p = _ln_call(x_prompt.reshape(bp * lp_, d), ln_in_g, ln_in_b)
    h_s = _ln_call(x_sample.reshape(bs * ls, d), ln_in_g, ln_in_b)
    new_p, new_s = [], []
    for i in range(depth):
        lp = _prep_layer(*[t[i] for t in layer_params])
        h_p, *st_p = _layer(h_p, *_zero_states(bp), lp, bp, lp_)
        new_p.append(st_p)
        h_s, *st_s = _layer(h_s, state_gla[i], state_rwkv[i], state_rwkv_shift[i], state_ssm[i],
                            state_ssm_conv[i], lp, bs, ls)
        new_s.append(st_s)
    stack = lambda rows: tuple(jnp.stack(z) for z in zip(*rows))
    return (h_p.reshape(bp, lp_, d), h_s.reshape(bs, ls, d)) + stack(new_p) + stack(new_s)
```
